```python
import math
import jax, jax.numpy as jnp
from jax import lax
import numpy as np

D_MODEL = 2048
BATCH = 4
SEQ = 8192
DEPTH = 2
DEC_BATCH = 8
DEC_SEQ = 2048
PAST_LEN = 128

W_BRANCH = D_MODEL // 2
W_A = W_BRANCH
W_B = W_BRANCH
W_C = W_BRANCH
W_D = W_BRANCH
N_BRANCH = 4
CONV_WIDTH = 31
CONV_PAD = CONV_WIDTH // 2
POOL_WINDOWS = (2, 4, 8, 16)
POOL_GROUP = W_B // len(POOL_WINDOWS)
HG_HEAD_DIM = 128
HG_HEADS = W_C // HG_HEAD_DIM
HG_CHUNK = 64
S5_GROUP_CH = 16
S5_GROUPS = W_D // S5_GROUP_CH
S5_STATE = 64
DT_MIN = 0.001
DT_MAX = 0.1
EPS = 1e-6
IN_SIZES = (W_A, W_A, W_A, W_B, W_B, W_C, W_C, W_C, W_C, W_C, W_D, W_D, N_BRANCH * D_MODEL)
N_IN = 3 * W_A + 2 * W_B + 5 * W_C + 2 * W_D + N_BRANCH * D_MODEL

kernel_name = 'hybrid_bidir_gated_encoder'


def _rms_norm(x, g):
    xf = x.astype(jnp.float32)
    y = xf * lax.rsqrt(jnp.mean(xf * xf, axis=-1, keepdims=True) + EPS) * g.astype(jnp.float32)
    return y.astype(x.dtype)


def _conformer_conv(val, glu, conv_w, conv_b, ln_g, ln_b):
    u = val * jax.nn.sigmoid(glu)
    y = lax.conv_general_dilated(u, conv_w[:, None, :].astype(u.dtype), window_strides=(1,),
                                 padding=((CONV_PAD, CONV_PAD),),
                                 dimension_numbers=('NWC', 'WIO', 'NWC'),
                                 feature_group_count=W_A)
    yf = y.astype(jnp.float32) + conv_b.astype(jnp.float32)
    mu = jnp.mean(yf, axis=-1, keepdims=True)
    var = jnp.mean(jnp.square(yf - mu), axis=-1, keepdims=True)
    yn = (yf - mu) * lax.rsqrt(var + EPS) * ln_g.astype(jnp.float32) + ln_b.astype(jnp.float32)
    return jax.nn.silu(yn).astype(val.dtype)


def _pool_mixer(u, pool_w, pool_scale):
    bsz, L, _ = u.shape
    uf = u.astype(jnp.float32)
    cs = jnp.concatenate([jnp.zeros((bsz, 1, W_B), jnp.float32), jnp.cumsum(uf, axis=1)], axis=1)
    t = jnp.arange(L)
    means = []
    for g, win in enumerate(POOL_WINDOWS):
        left = win // 2
        right = win - 1 - left
        lo = jnp.maximum(t - left, 0)
        hi = jnp.minimum(t + right, L - 1) + 1
        csg = cs[:, :, g * POOL_GROUP:(g + 1) * POOL_GROUP]
        cnt = (hi - lo).astype(jnp.float32)[None, :, None]
        means.append((jnp.take(csg, hi, axis=1) - jnp.take(csg, lo, axis=1)) / cnt)
    d = (jnp.concatenate(means, axis=-1) - uf).reshape(bsz, L, len(POOL_WINDOWS), POOL_GROUP)
    y = jnp.einsum('blgc,gce->blge', d, pool_w.astype(jnp.float32)).reshape(bsz, L, W_B)
    return (y * pool_scale.astype(jnp.float32)).astype(u.dtype)


def _hgrn2_chunk_scan(q, log_f, k, v):
    bsz, L, H, K = q.shape
    V = v.shape[-1]
    n = L // HG_CHUNK

    def to_chunks(t):
        return t.reshape(bsz, n, HG_CHUNK, H, t.shape[-1]).transpose(1, 0, 3, 2, 4)

    qc, fc, kc, vc = to_chunks(q), to_chunks(log_f), to_chunks(k), to_chunks(v)
    lower = jnp.tril(jnp.ones((HG_CHUNK, HG_CHUNK), dtype=bool))[:, :, None]

    def step(S, inp):
        q_, lf_, k_, v_ = inp
        b = jnp.cumsum(lf_, axis=2)
        o_inter = jnp.einsum('bhck,bhkv->bhcv', q_ * jnp.exp(b), S)
        diff = b[:, :, :, None, :] - b[:, :, None, :, :]
        decay = jnp.exp(jnp.where(lower, diff, -jnp.inf))
        scores = jnp.einsum('bhtk,bhtsk,bhsk->bhts', q_, decay, k_)
        o_intra = jnp.einsum('bhts,bhsv->bhtv', scores, v_)
        b_end = b[:, :, -1:, :]
        S_new = jnp.exp(b_end[:, :, 0, :])[..., None] * S + jnp.einsum('bhsk,bhsv->bhkv', k_ * jnp.exp(b_end - b), v_)
        return S_new, o_inter + o_intra

    S0 = jnp.zeros((bsz, H, K, V), jnp.float32)
    _, o = lax.scan(step, S0, (qc, fc, kc, vc))
    return o.transpose(1, 0, 3, 2, 4).reshape(bsz, L, H, V)


def _hgrn2_mixer(q, zf_fwd, zf_bwd, v, lb, norm_g):
    bsz, L, _ = q.shape

    def heads(t):
        return t.astype(jnp.float32).reshape(bsz, L, HG_HEADS, HG_HEAD_DIM)

    qh, vh = heads(q), heads(v)
    lbh = lb.astype(jnp.float32).reshape(2, HG_HEADS, HG_HEAD_DIM)

    def gates(z, lbd):
        zh = heads(z)
        log_f = jnp.log(lbd + (1.0 - lbd) * jax.nn.sigmoid(zh))
        k = (1.0 - lbd) * jax.nn.sigmoid(-zh)
        return log_f, k

    lf_f, k_f = gates(zf_fwd, lbh[0])
    lf_b, k_b = gates(zf_bwd, lbh[1])
    o_fwd = _hgrn2_chunk_scan(qh, lf_f, k_f, vh)
    o_bwd = jnp.flip(_hgrn2_chunk_scan(jnp.flip(qh, 1), jnp.flip(lf_b, 1), jnp.flip(k_b, 1), jnp.flip(vh, 1)), 1)
    o = o_fwd + o_bwd
    o = o * lax.rsqrt(jnp.mean(o * o, axis=-1, keepdims=True) + EPS)
    o = o.reshape(bsz, L, W_C) * norm_g.astype(jnp.float32)
    return o.astype(q.dtype)


def _complex_affine_combine(e1, e2):
    ar1, ai1, br1, bi1 = e1
    ar2, ai2, br2, bi2 = e2
    ar = ar1 * ar2 - ai1 * ai2
    ai = ar1 * ai2 + ai1 * ar2
    br = ar2 * br1 - ai2 * bi1 + br2
    bi = ar2 * bi1 + ai2 * br1 + bi2
    return (ar, ai, br, bi)


def _s5_direction(ug, a_re, a_im, log_dt, b_re, b_im, c_re, c_im, reverse):
    f32 = jnp.float32
    a_re, a_im = a_re.astype(f32), a_im.astype(f32)
    b_re, b_im = b_re.astype(f32), b_im.astype(f32)
    dt = jnp.exp(log_dt.astype(f32))[:, None]
    mag = jnp.exp(dt * a_re)
    ang = dt * a_im
    ab_re, ab_im = mag * jnp.cos(ang), mag * jnp.sin(ang)
    den = a_re * a_re + a_im * a_im
    x_, y_ = ab_re - 1.0, ab_im
    g_re = (x_ * a_re + y_ * a_im) / den
    g_im = (y_ * a_re - x_ * a_im) / den
    bb_re = g_re[..., None] * b_re - g_im[..., None] * b_im
    bb_im = g_re[..., None] * b_im + g_im[..., None] * b_re
    bu_re = jnp.einsum('gpc,blgc->blgp', bb_re, ug)
    bu_im = jnp.einsum('gpc,blgc->blgp', bb_im, ug)
    L = ug.shape[1]
    ar = jnp.broadcast_to(ab_re[None, None], (1, L) + ab_re.shape)
    ai = jnp.broadcast_to(ab_im[None, None], (1, L) + ab_im.shape)
    _, _, h_re, h_im = lax.associative_scan(_complex_affine_combine, (ar, ai, bu_re, bu_im), axis=1, reverse=reverse)
    return (jnp.einsum('gcp,blgp->blgc', c_re.astype(f32), h_re)
            - jnp.einsum('gcp,blgp->blgc', c_im.astype(f32), h_im))


def _s5_mixer(u, a_re, a_im, log_dt, b_re, b_im, c_re, c_im, d_skip, glu_w, glu_b):
    bsz, L, _ = u.shape
    uf = u.astype(jnp.float32)
    ug = uf.reshape(bsz, L, S5_GROUPS, S5_GROUP_CH)
    y = uf * d_skip.astype(jnp.float32)
    for d in range(2):
        y = y + _s5_direction(ug, a_re[d], a_im[d], log_dt[d], b_re[d], b_im[d], c_re[d], c_im[d],
                              reverse=(d == 1)).reshape(bsz, L, W_D)
    z = jax.nn.gelu(y)
    z = z * jax.nn.sigmoid(z @ glu_w.astype(jnp.float32) + glu_b.astype(jnp.float32))
    return z.astype(u.dtype)


def _layer(x, lb, norm_g, w_in, conv_w, conv_b, conv_ln_g, conv_ln_b, w_a_out, pool_w, pool_scale, w_b_out,
           hg_norm_g, w_c_out, s5_a_re, s5_a_im, s5_log_dt, s5_b_re, s5_b_im, s5_c_re, s5_c_im, s5_d,
           s5_glu_w, s5_glu_b, w_d_out, w_out):
    bsz, L, _ = x.shape
    h = _rms_norm(x, norm_g)
    p = jnp.einsum('bld,dn->bln', h, w_in.astype(h.dtype))
    split_points = np.cumsum(np.array(IN_SIZES))[:-1].tolist()
    (a_val, a_glu, a_gate, b_in, b_gate, c_q, c_ff, c_fb, c_i, c_gate,
     d_in, d_gate, r) = jnp.split(p, split_points, axis=-1)
    ya = _conformer_conv(a_val, a_glu, conv_w, conv_b, conv_ln_g, conv_ln_b) * jax.nn.silu(a_gate)
    yb = _pool_mixer(b_in, pool_w, pool_scale) * jax.nn.silu(b_gate)
    yc = _hgrn2_mixer(c_q, c_ff, c_fb, c_i, lb, hg_norm_g) * jax.nn.silu(c_gate)
    yd = _s5_mixer(d_in, s5_a_re, s5_a_im, s5_log_dt, s5_b_re, s5_b_im, s5_c_re, s5_c_im, s5_d,
                   s5_glu_w, s5_glu_b) * jax.nn.silu(d_gate)
    gates = jax.nn.sigmoid(r.reshape(bsz, L, N_BRANCH, D_MODEL))
    m = (gates[:, :, 0] * (ya @ w_a_out.astype(ya.dtype))
         + gates[:, :, 1] * (yb @ w_b_out.astype(yb.dtype))
         + gates[:, :, 2] * (yc @ w_c_out.astype(yc.dtype))
         + gates[:, :, 3] * (yd @ w_d_out.astype(yd.dtype)))
    return x + m @ w_out.astype(m.dtype)


def setup_inputs(seed: int = 0) -> dict:
    key = jax.random.key(seed)
    ks = jax.random.split(key, 40)
    counter = [0]

    def nxt():
        k = ks[counter[0]]
        counter[0] += 1
        return k

    def nrm(shape, scale):
        return jax.random.normal(nxt(), shape, jnp.float32) * scale

    n_idx = jnp.arange(S5_STATE, dtype=jnp.float32)
    return {
        'x_prompt': nrm((BATCH, SEQ, D_MODEL), 1.0),
        'x_sample': nrm((DEC_BATCH, DEC_SEQ, D_MODEL), 1.0),
        'norm_g': 1.0 + nrm((DEPTH, D_MODEL), 0.02),
        'w_in': nrm((DEPTH, D_MODEL, N_IN), D_MODEL ** -0.5),
        'conv_w': nrm((DEPTH, CONV_WIDTH, W_A), CONV_WIDTH ** -0.5),
        'conv_b': nrm((DEPTH, W_A), 0.02),
        'conv_ln_g': 1.0 + nrm((DEPTH, W_A), 0.02),
        'conv_ln_b': nrm((DEPTH, W_A), 0.02),
        'w_a_out': nrm((DEPTH, W_A, D_MODEL), W_A ** -0.5),
        'pool_w': nrm((DEPTH, len(POOL_WINDOWS), POOL_GROUP, POOL_GROUP), POOL_GROUP ** -0.5),
        'pool_scale': 1.0 + nrm((DEPTH, W_B), 0.02),
        'w_b_out': nrm((DEPTH, W_B, D_MODEL), W_B ** -0.5),
        'hg_lb': nrm((DEPTH, 2, W_C), 1.0),
        'hg_norm_g': 1.0 + nrm((DEPTH, W_C), 0.02),
        'w_c_out': nrm((DEPTH, W_C, D_MODEL), W_C ** -0.5),
        's5_a_re': -0.5 + nrm((DEPTH, 2, S5_GROUPS, S5_STATE), 0.01),
        's5_a_im': math.pi * n_idx + nrm((DEPTH, 2, S5_GROUPS, S5_STATE), 0.01),
        's5_log_dt': math.log(DT_MIN) + (math.log(DT_MAX) - math.log(DT_MIN))
                     * jax.random.uniform(nxt(), (DEPTH, 2, S5_GROUPS), jnp.float32),
        's5_b_re': nrm((DEPTH, 2, S5_GROUPS, S5_STATE, S5_GROUP_CH), (2 * S5_GROUP_CH) ** -0.5),
        's5_b_im': nrm((DEPTH, 2, S5_GROUPS, S5_STATE, S5_GROUP_CH), (2 * S5_GROUP_CH) ** -0.5),
        's5_c_re': nrm((DEPTH, 2, S5_GROUPS, S5_GROUP_CH, S5_STATE), S5_STATE ** -0.5),
        's5_c_im': nrm((DEPTH, 2, S5_GROUPS, S5_GROUP_CH, S5_STATE), S5_STATE ** -0.5),
        's5_d': nrm((DEPTH, W_D), 1.0),
        's5_glu_w': nrm((DEPTH, W_D, W_D), W_D ** -0.5),
        's5_glu_b': nrm((DEPTH, W_D), 0.02),
        'w_d_out': nrm((DEPTH, W_D, D_MODEL), W_D ** -0.5),
        'w_out': nrm((DEPTH, D_MODEL, D_MODEL), D_MODEL ** -0.5),
        'final_g': 1.0 + nrm((D_MODEL,), 0.02),
    }


def reference(x_prompt, x_sample, norm_g, w_in, conv_w, conv_b, conv_ln_g, conv_ln_b, w_a_out, pool_w,
              pool_scale, w_b_out, hg_lb, hg_norm_g, w_c_out, s5_a_re, s5_a_im, s5_log_dt, s5_b_re, s5_b_im,
              s5_c_re, s5_c_im, s5_d, s5_glu_w, s5_glu_b, w_d_out, w_out, final_g):
    lb_all = jnp.cumsum(jax.nn.softmax(hg_lb.astype(jnp.float32), axis=0), axis=0)
    lb_all = lb_all - lb_all[:1]
    layered = (norm_g, w_in, conv_w, conv_b, conv_ln_g, conv_ln_b, w_a_out, pool_w, pool_scale, w_b_out,
               hg_norm_g, w_c_out, s5_a_re, s5_a_im, s5_log_dt, s5_b_re, s5_b_im, s5_c_re, s5_c_im, s5_d,
               s5_glu_w, s5_glu_b, w_d_out, w_out)

    def run(x):
        for l in range(DEPTH):
            x = _layer(x, lb_all[l], *[w[l] for w in layered])
        return _rms_norm(x, final_g)

    y_prompt = run(x_prompt)
    y_sample = run(x_sample)
    return (y_prompt, y_sample)
```

```python
import functools
import math

import numpy as np
import jax
import jax.numpy as jnp
from jax import lax
from jax.experimental import pallas as pl
from jax.experimental.pallas import tpu as pltpu

F32 = jnp.float32
BF16 = jnp.bfloat16

D_MODEL = 2048
DEPTH = 2
W_BRANCH = 1024
N_BRANCH = 4
N_IN = 12 * W_BRANCH + N_BRANCH * D_MODEL
CONV_WIDTH = 31
CONV_PAD = CONV_WIDTH // 2
POOL_WINDOWS = (2, 4, 8, 16)
POOL_GROUP = W_BRANCH // len(POOL_WINDOWS)
HG_HEAD_DIM = 128
HG_HEADS = W_BRANCH // HG_HEAD_DIM
S5_GROUP_CH = 16
S5_GROUPS = W_BRANCH // S5_GROUP_CH
S5_STATE = 64
EPS = 1e-6

LANES = 128
SUBLANES = 8
VMEM_LIMIT = 52 * 1024 * 1024

COL_A_VAL, COL_A_GLU, COL_A_GATE, COL_B_IN, COL_B_GATE = 0, 1, 2, 3, 4
COL_C_Q, COL_C_FF, COL_C_FB, COL_C_I, COL_C_GATE, COL_D_IN, COL_D_GATE = 5, 6, 7, 8, 9, 10, 11
COL_R = 12

HALO = 16
CONV_LANES = 256
HG_CHUNK = 64
HG_LEVELS = 6
HG_NROWS = (HG_LEVELS + 2) * HG_CHUNK
S5_TILE = 512
S5_SUB = S5_TILE // SUBLANES
S5_LT = W_BRANCH // LANES
S5_GPT = LANES // S5_GROUP_CH
S5_SW = S5_GPT * S5_STATE


def _sigmoid(x):
    return 1.0 / (1.0 + jnp.exp(-x))


def _silu(x):
    return x * _sigmoid(x)


def _cparams(sem):
    return pltpu.CompilerParams(dimension_semantics=sem, vmem_limit_bytes=VMEM_LIMIT)


def _in_proj_kernel(x_ref, g_ref, w_ref, o_ref, h_ref):
    @pl.when(pl.program_id(1) == 0)
    def _():
        x = x_ref[...]
        ms = jnp.mean(x * x, axis=-1, keepdims=True)
        h_ref[...] = (x * lax.rsqrt(ms + EPS) * g_ref[...]).astype(BF16)

    o_ref[...] = jnp.dot(h_ref[...], w_ref[...], preferred_element_type=F32)


def _in_proj(x2, g, w_bf, tm=1024, tn=1024):
    t = x2.shape[0]
    return pl.pallas_call(
        _in_proj_kernel,
        grid=(t // tm, N_IN // tn),
        in_specs=[pl.BlockSpec((tm, D_MODEL), lambda i, j: (i, 0)),
                  pl.BlockSpec((1, D_MODEL), lambda i, j: (0, 0)),
                  pl.BlockSpec((D_MODEL, tn), lambda i, j: (0, j))],
        out_specs=pl.BlockSpec((tm, tn), lambda i, j: (i, j)),
        out_shape=jax.ShapeDtypeStruct((t, N_IN), F32),
        scratch_shapes=[pltpu.VMEM((tm, D_MODEL), BF16)],
        compiler_params=_cparams(("parallel", "arbitrary")),
        name="in_proj",
    )(x2, g, w_bf)


def _conv_kernel(main_ref, gate_ref, prev_ref, next_ref, cw_ref, cb_ref, lg_ref, lb_ref, o_ref, u_scr,
                 *, tt, rc):
    t = pl.program_id(1)
    n_t = pl.num_programs(1)

    def glu(ref):
        return ref[0, :, :W_BRANCH] * _sigmoid(ref[0, :, W_BRANCH:])

    u_scr[HALO:HALO + tt, :] = glu(main_ref)
    u_scr[0:HALO, :] = jnp.where(t > 0, glu(prev_ref), 0.0)
    u_scr[HALO + tt:2 * HALO + tt, :] = jnp.where(t < n_t - 1, glu(next_ref), 0.0)

    def body(c, carry):
        r0 = pl.multiple_of(c * rc, rc)
        parts = []
        for lc in range(W_BRANCH // CONV_LANES):
            lanes = slice(lc * CONV_LANES, (lc + 1) * CONV_LANES)
            win = u_scr[pl.ds(r0, rc + 2 * HALO), lanes]
            acc = jnp.zeros((rc, CONV_LANES), F32)
            for j in range(CONV_WIDTH):
                off = HALO - CONV_PAD + j
                acc = acc + win[off:off + rc, :] * cw_ref[j:j + 1, lanes]
            parts.append(acc)
        y = jnp.concatenate(parts, axis=-1) + cb_ref[...]
        mu = jnp.mean(y, axis=-1, keepdims=True)
        yc = y - mu
        var = jnp.mean(yc * yc, axis=-1, keepdims=True)
        yn = yc * lax.rsqrt(var + EPS) * lg_ref[...] + lb_ref[...]
        out = _silu(yn) * _silu(gate_ref[0, pl.ds(r0, rc), :])
        o_ref[0, pl.ds(r0, rc), :] = out.astype(o_ref.dtype)
        return carry

    lax.fori_loop(0, tt // rc, body, 0)


def _halo_specs(tt, width, col_blk, seq_len):
    per = tt // HALO
    last = seq_len // HALO - 1
    prev = pl.BlockSpec((1, HALO, width), lambda b, t: (b, jnp.maximum(t * per - 1, 0), col_blk))
    nxt = pl.BlockSpec((1, HALO, width), lambda b, t: (b, jnp.minimum((t + 1) * per, last), col_blk))
    return prev, nxt


def _conv_branch(p3, cw, cb, lg, lb, tt=512, rc=16):
    bsz, seq_len, _ = p3.shape
    prev, nxt = _halo_specs(tt, 2 * W_BRANCH, 0, seq_len)
    vec = pl.BlockSpec((1, W_BRANCH), lambda b, t: (0, 0))
    return pl.pallas_call(
        functools.partial(_conv_kernel, tt=tt, rc=rc),
        grid=(bsz, seq_len // tt),
        in_specs=[pl.BlockSpec((1, tt, 2 * W_BRANCH), lambda b, t: (b, t, 0)),
                  pl.BlockSpec((1, tt, W_BRANCH), lambda b, t: (b, t, COL_A_GATE)),
                  prev, nxt,
                  pl.BlockSpec((CONV_WIDTH, W_BRANCH), lambda b, t: (0, 0)),
                  vec, vec, vec],
        out_specs=pl.BlockSpec((1, tt, W_BRANCH), lambda b, t: (b, t, 0)),
        out_shape=jax.ShapeDtypeStruct((bsz, seq_len, W_BRANCH), BF16),
        scratch_shapes=[pltpu.VMEM((tt + 2 * HALO, W_BRANCH), F32)],
        compiler_params=_cparams(("parallel", "parallel")),
        name="conv_branch",
    )(p3, p3, p3, p3, cw, cb, lg, lb)


def _pool_kernel(main_ref, gate_ref, prev_ref, next_ref, pw_ref, ps_ref, o_ref, e_scr, s_scr,
                 *, tt, seq_len):
    t = pl.program_id(1)
    n_t = pl.num_programs(1)
    rows = tt + 2 * HALO
    e_scr[HALO:HALO + tt, :] = main_ref[0]
    e_scr[0:HALO, :] = jnp.where(t > 0, prev_ref[0], 0.0)
    e_scr[HALO + tt:rows, :] = jnp.where(t < n_t - 1, next_ref[0], 0.0)

    tpos = t * tt + lax.broadcasted_iota(jnp.int32, (tt, 1), 0)
    for g, win in enumerate(POOL_WINDOWS):
        lanes = slice(g * POOL_GROUP, (g + 1) * POOL_GROUP)
        n = rows - 2
        s_scr[1:1 + n, lanes] = e_scr[1:1 + n, lanes] + e_scr[0:n, lanes]
        half = 1
        while 2 * half < win:
            n = rows - 2 * half - 2 * half
            lo = 2 * half
            a = s_scr[lo + half:lo + half + n, lanes]
            b = s_scr[lo - half:lo - half + n, lanes]
            s_scr[lo:lo + n, lanes] = a + b
            half *= 2
        left = win // 2
        right = win - 1 - left
        cnt = (jnp.minimum(tpos + right, seq_len - 1) + 1 - jnp.maximum(tpos - left, 0)).astype(F32)
        u = e_scr[HALO:HALO + tt, lanes]
        d = s_scr[HALO:HALO + tt, lanes] / cnt - u
        y = jnp.dot(d.astype(BF16), pw_ref[g], preferred_element_type=F32)
        y = y * ps_ref[:, lanes] * _silu(gate_ref[0, :, lanes])
        o_ref[0, :, lanes] = y.astype(o_ref.dtype)


def _pool_branch(p3, pw_bf, ps, tt=512):
    bsz, seq_len, _ = p3.shape
    prev, nxt = _halo_specs(tt, W_BRANCH, COL_B_IN, seq_len)
    return pl.pallas_call(
        functools.partial(_pool_kernel, tt=tt, seq_len=seq_len),
        grid=(bsz, seq_len // tt),
        in_specs=[pl.BlockSpec((1, tt, W_BRANCH), lambda b, t: (b, t, COL_B_IN)),
                  pl.BlockSpec((1, tt, W_BRANCH), lambda b, t: (b, t, COL_B_GATE)),
                  prev, nxt,
                  pl.BlockSpec((len(POOL_WINDOWS), POOL_GROUP, POOL_GROUP), lambda b, t: (0, 0, 0)),
                  pl.BlockSpec((1, W_BRANCH), lambda b, t: (0, 0))],
        out_specs=pl.BlockSpec((1, tt, W_BRANCH), lambda b, t: (b, t, 0)),
        out_shape=jax.ShapeDtypeStruct((bsz, seq_len, W_BRANCH), BF16),
        scratch_shapes=[pltpu.VMEM((tt + 2 * HALO, W_BRANCH), F32),
                        pltpu.VMEM((tt + 2 * HALO, W_BRANCH), F32)],
        compiler_params=_cparams(("parallel", "parallel")),
        name="pool_branch",
    )(p3, p3, p3, p3, pw_bf, ps)


def _hg_tables():
    c = HG_CHUNK
    t = np.arange(c)
    r = np.arange(c)
    nmat = np.zeros((HG_LEVELS + 2, c, c), np.float32)
    masks = np.zeros((HG_LEVELS + 1, c, c), np.float32)
    nmat[0] = (r[None, :] <= t[:, None])
    nmat[1] = (r[None, :] > t[:, None])
    masks[0] = np.eye(c)
    for lvl in range(HG_LEVELS):
        half = 1 << lvl
        blk = t // (2 * half)
        mid = blk * 2 * half + half
        later = (t % (2 * half)) >= half
        rr = r[None, :]
        q_side = later[:, None] & (rr >= mid[:, None]) & (rr <= t[:, None])
        k_side = (~later)[:, None] & (rr >= t[:, None] + 1) & (rr <= mid[:, None] - 1)
        nmat[2 + lvl] = q_side | k_side
        masks[1 + lvl] = (blk[:, None] == blk[None, :]) & later[:, None] & (~later)[None, :]
    nmat_b = nmat[:, ::-1, ::-1].copy()
    masks_b = masks[:, ::-1, ::-1].copy()
    nm = np.stack([nmat.reshape(HG_NROWS, c), nmat_b.reshape(HG_NROWS, c)])
    mk = np.stack([masks, masks_b])
    return nm, mk


def _dot_nt(a, b):
    return lax.dot_general(a, b, (((1,), (1,)), ((), ())), preferred_element_type=F32)


def _hg_chunk(q, z, v, lbv, nmat, mask_ref, d, st, end_row):
    c = HG_CHUNK
    f = lbv + (1.0 - lbv) * _sigmoid(z)
    lf = jnp.log(f)
    kk = (1.0 - lbv) * _sigmoid(-z)
    hi = lf.astype(BF16)
    lo = (lf - hi.astype(F32)).astype(BF16)
    dall = (jnp.dot(nmat, hi, preferred_element_type=F32)
            + jnp.dot(nmat, lo, preferred_element_type=F32))
    eall = jnp.exp(dall)
    e_b = eall[0:c]
    e_e = eall[c:2 * c]
    bend = e_b[end_row:end_row + 1]
    qb = q.astype(BF16)
    kb = kk.astype(BF16)
    vb = v.astype(BF16)
    a = mask_ref[d, 0] * _dot_nt(qb, kb)
    for lvl in range(HG_LEVELS):
        el = eall[(2 + lvl) * c:(3 + lvl) * c]
        a = a + mask_ref[d, 1 + lvl] * _dot_nt((q * el).astype(BF16), (kk * el).astype(BF16))
    o = (jnp.dot(a.astype(BF16), vb, preferred_element_type=F32)
         + _dot_nt((q * e_b).astype(BF16), st.astype(BF16)))
    vt = jnp.transpose(v).astype(BF16)
    st_new = st * bend + jnp.dot(vt, (kk * e_e).astype(BF16), preferred_element_type=F32)
    return o, st_new


def _hg_kernel(qf_ref, zf_ref, vf_ref, qb_ref, zb_ref, vb_ref, lb_ref, nm_ref, mk_ref,
               of_ref, ob_ref, sf_scr, sb_scr, *, tt):
    @pl.when(pl.program_id(2) == 0)
    def _():
        sf_scr[...] = jnp.zeros_like(sf_scr)
        sb_scr[...] = jnp.zeros_like(sb_scr)

    n_c = tt // HG_CHUNK
    lbf = lb_ref[0:1, :]
    lbb = lb_ref[1:2, :]
    sf = sf_scr[...]
    sb = sb_scr[...]
    for ci in range(n_c):
        rows = pl.ds(ci * HG_CHUNK, HG_CHUNK)
        o, sf = _hg_chunk(qf_ref[0, rows, :], zf_ref[0, rows, :], vf_ref[0, rows, :], lbf,
                          nm_ref[0], mk_ref, 0, sf, HG_CHUNK - 1)
        of_ref[0, rows, :] = o
        rows = pl.ds((n_c - 1 - ci) * HG_CHUNK, HG_CHUNK)
        o, sb = _hg_chunk(qb_ref[0, rows, :], zb_ref[0, rows, :], vb_ref[0, rows, :], lbb,
                          nm_ref[1], mk_ref, 1, sb, 0)
        ob_ref[0, rows, :] = o
    sf_scr[...] = sf
    sb_scr[...] = sb


def _hg_scan(p3, lb, nm_bf, mk, tt=256):
    bsz, seq_len, _ = p3.shape
    n_t = seq_len // tt
    cpb = W_BRANCH // HG_HEAD_DIM

    def fspec(col):
        return pl.BlockSpec((1, tt, HG_HEAD_DIM), lambda b, h, k: (b, k, col * cpb + h))

    def bspec(col):
        return pl.BlockSpec((1, tt, HG_HEAD_DIM), lambda b, h, k: (b, n_t - 1 - k, col * cpb + h))

    out = jax.ShapeDtypeStruct((bsz, seq_len, W_BRANCH), F32)
    return pl.pallas_call(
        functools.partial(_hg_kernel, tt=tt),
        grid=(bsz, HG_HEADS, n_t),
        in_specs=[fspec(COL_C_Q), fspec(COL_C_FF), fspec(COL_C_I),
                  bspec(COL_C_Q), bspec(COL_C_FB), bspec(COL_C_I),
                  pl.BlockSpec((2, HG_HEAD_DIM), lambda b, h, k: (0, h)),
                  pl.BlockSpec((2, HG_NROWS, HG_CHUNK), lambda b, h, k: (0, 0, 0)),
                  pl.BlockSpec((2, HG_LEVELS + 1, HG_CHUNK, HG_CHUNK), lambda b, h, k: (0, 0, 0, 0))],
        out_specs=[pl.BlockSpec((1, tt, HG_HEAD_DIM), lambda b, h, k: (b, k, h)),
                   pl.BlockSpec((1, tt, HG_HEAD_DIM), lambda b, h, k: (b, n_t - 1 - k, h))],
        out_shape=[out, out],
        scratch_shapes=[pltpu.VMEM((HG_HEAD_DIM, HG_HEAD_DIM), F32),
                        pltpu.VMEM((HG_HEAD_DIM, HG_HEAD_DIM), F32)],
        compiler_params=_cparams(("parallel", "parallel", "arbitrary")),
        name="hgrn2_scan",
    )(p3, p3, p3, p3, p3, p3, lb, nm_bf, mk)


def _hg_post_kernel(of_ref, ob_ref, gate_ref, g_ref, o_ref):
    for h in range(HG_HEADS):
        lanes = slice(h * HG_HEAD_DIM, (h + 1) * HG_HEAD_DIM)
        o = of_ref[0, :, lanes] + ob_ref[0, :, lanes]
        o = o * lax.rsqrt(jnp.mean(o * o, axis=-1, keepdims=True) + EPS)
        y = o * g_ref[:, lanes] * _silu(gate_ref[0, :, lanes])
        o_ref[0, :, lanes] = y.astype(o_ref.dtype)


def _hg_post(o_f, o_b, p3, norm_g, tt=512):
    bsz, seq_len, _ = p3.shape
    blk = pl.BlockSpec((1, tt, W_BRANCH), lambda b, t: (b, t, 0))
    return pl.pallas_call(
        _hg_post_kernel,
        grid=(bsz, seq_len // tt),
        in_specs=[blk, blk,
                  pl.BlockSpec((1, tt, W_BRANCH), lambda b, t: (b, t, COL_C_GATE)),
                  pl.BlockSpec((1, W_BRANCH), lambda b, t: (0, 0))],
        out_specs=blk,
        out_shape=jax.ShapeDtypeStruct((bsz, seq_len, W_BRANCH), BF16),
        compiler_params=_cparams(("parallel", "parallel")),
        name="hgrn2_post",
    )(o_f, o_b, p3, norm_g)


def _s5_prep_kernel(are_ref, aim_ref, ldt_ref, bre_ref, bim_ref,
                    abr_ref, abi_ref, bbr_ref, bbi_ref, pwr_ref, pwi_ref):
    a_re = are_ref[...]
    a_im = aim_ref[...]
    dt = jnp.exp(ldt_ref[...])
    mag = jnp.exp(dt * a_re)
    ang = dt * a_im
    ab_re = mag * jnp.cos(ang)
    ab_im = mag * jnp.sin(ang)
    den = a_re * a_re + a_im * a_im
    x_ = ab_re - 1.0
    y_ = ab_im
    g_re = (x_ * a_re + y_ * a_im) / den
    g_im = (y_ * a_re - x_ * a_im) / den
    abr_ref[...] = ab_re
    abi_ref[...] = ab_im
    for c in range(S5_GROUP_CH):
        b_re = bre_ref[c]
        b_im = bim_ref[c]
        bbr_ref[c] = g_re * b_re - g_im * b_im
        bbi_ref[c] = g_re * b_im + g_im * b_re
    p_re = ab_re
    p_im = ab_im
    for i in range(S5_SUB):
        pwr_ref[i] = p_re
        pwi_ref[i] = p_im
        p_re, p_im = p_re * ab_re - p_im * ab_im, p_re * ab_im + p_im * ab_re


def _s5_prep(a_re, a_im, log_dt, b_re, b_im):
    n = 2 * S5_GROUPS
    sp = jax.ShapeDtypeStruct((n, S5_STATE), F32)
    sb = jax.ShapeDtypeStruct((S5_GROUP_CH, n, S5_STATE), F32)
    spw = jax.ShapeDtypeStruct((S5_SUB, n, S5_STATE), F32)
    return pl.pallas_call(
        _s5_prep_kernel,
        out_shape=[sp, sp, sb, sb, spw, spw],
        name="s5_prep",
    )(a_re.reshape(n, S5_STATE), a_im.reshape(n, S5_STATE), log_dt.reshape(n, 1),
      jnp.transpose(b_re.reshape(n, S5_STATE, S5_GROUP_CH), (2, 0, 1)),
      jnp.transpose(b_im.reshape(n, S5_STATE, S5_GROUP_CH), (2, 0, 1)))


def _s5_layout(abr, abi, bbr, bbi, pwr, pwi, c_re, c_im):
    eye = jnp.eye(S5_GPT, dtype=F32)

    def tiles(x):
        return x.reshape(2, S5_LT, 1, S5_SW)

    a_t = jnp.stack([tiles(abr), tiles(abi)], axis=1)

    def pw_tiles(x):
        return jnp.transpose(x.reshape(S5_SUB, 2, S5_LT, S5_SW), (1, 2, 0, 3))

    pw_t = jnp.stack([pw_tiles(pwr), pw_tiles(pwi)], axis=1)

    def b_bd(x):
        x = jnp.transpose(x, (1, 0, 2)).reshape(2, S5_LT, S5_GPT, S5_GROUP_CH, S5_STATE)
        bd = x[:, :, :, :, None, :] * eye[None, None, :, None, :, None]
        return bd.reshape(2, S5_LT, LANES, S5_SW)

    b_t = jnp.concatenate([b_bd(bbr), b_bd(bbi)], axis=-1).astype(BF16)

    def c_bd(x):
        x = jnp.transpose(x.reshape(2, S5_LT, S5_GPT, S5_GROUP_CH, S5_STATE), (0, 1, 2, 4, 3))
        bd = x[:, :, :, :, None, :] * eye[None, None, :, None, :, None]
        return bd.reshape(2, S5_LT, S5_SW, LANES).astype(BF16)

    return a_t, pw_t, b_t, c_bd(c_re), c_bd(c_im)


def _s5_kernel(u_ref, a_ref, pw_ref, b_ref, cre_ref, cim_ref, d_ref, o_ref,
               up_scr, hre_scr, him_scr, hin_re_scr, hin_im_scr, *, seq_len):
    n_t = seq_len // S5_TILE
    o_ref[0] = u_ref[0] * d_ref[...]
    zero_row = jnp.zeros((1, S5_SW), F32)
    zero_blk = jnp.zeros((SUBLANES, S5_SW), F32)

    for d in range(2):
        ar1 = a_ref[d, 0, 0]
        ai1 = a_ref[d, 1, 0]
        ar = jnp.broadcast_to(ar1, (SUBLANES, S5_SW))
        ai = jnp.broadcast_to(ai1, (SUBLANES, S5_SW))
        asr = pw_ref[d, 0, 0, S5_SUB - 1:S5_SUB, :]
        asi = pw_ref[d, 1, 0, S5_SUB - 1:S5_SUB, :]

        def tile_body(k, carry, d=d, ar=ar, ai=ai, asr=asr, asi=asi):
            c_re, c_im = carry
            kt = k if d == 0 else n_t - 1 - k
            t0 = pl.multiple_of(kt * S5_TILE, S5_TILE)
            for i in range(S5_SUB):
                up_scr[i * SUBLANES:(i + 1) * SUBLANES, :] = u_ref[0, pl.ds(t0 + i, SUBLANES, stride=S5_SUB), :]
            bu = jnp.dot(up_scr[...].astype(BF16), b_ref[d, 0], preferred_element_type=F32)
            hre_scr[...] = bu[:, :S5_SW]
            him_scr[...] = bu[:, S5_SW:]

            def step(i, h):
                ii = i if d == 0 else S5_SUB - 1 - i
                rows = pl.ds(pl.multiple_of(ii * SUBLANES, SUBLANES), SUBLANES)
                hr, hi = h
                nr = ar * hr - ai * hi + hre_scr[rows, :]
                ni = ar * hi + ai * hr + him_scr[rows, :]
                hre_scr[rows, :] = nr
                him_scr[rows, :] = ni
                return nr, ni

            end_re, end_im = lax.fori_loop(0, S5_SUB, step, (zero_blk, zero_blk))
            order = range(SUBLANES) if d == 0 else range(SUBLANES - 1, -1, -1)
            for j in order:
                hin_re_scr[j:j + 1, :] = c_re
                hin_im_scr[j:j + 1, :] = c_im
                er = end_re[j:j + 1, :]
                ei = end_im[j:j + 1, :]
                c_re, c_im = er + asr * c_re - asi * c_im, ei + asr * c_im + asi * c_re
            hin_re = hin_re_scr[...]
            hin_im = hin_im_scr[...]

            def fix(i, _):
                rows = pl.ds(pl.multiple_of(i * SUBLANES, SUBLANES), SUBLANES)
                pi = i if d == 0 else S5_SUB - 1 - i
                pr = jnp.broadcast_to(pw_ref[d, 0, 0, pl.ds(pi, 1), :], (SUBLANES, S5_SW))
                pim = jnp.broadcast_to(pw_ref[d, 1, 0, pl.ds(pi, 1), :], (SUBLANES, S5_SW))
                hre_scr[rows, :] = hre_scr[rows, :] + pr * hin_re - pim * hin_im
                him_scr[rows, :] = him_scr[rows, :] + pr * hin_im + pim * hin_re
                return 0

            lax.fori_loop(0, S5_SUB, fix, 0)
            y = (jnp.dot(hre_scr[...].astype(BF16), cre_ref[d, 0], preferred_element_type=F32)
                 - jnp.dot(him_scr[...].astype(BF16), cim_ref[d, 0], preferred_element_type=F32))
            for i in range(S5_SUB):
                rows = pl.ds(t0 + i, SUBLANES, stride=S5_SUB)
                o_ref[0, rows, :] = o_ref[0, rows, :] + y[i * SUBLANES:(i + 1) * SUBLANES, :]
            return c_re, c_im

        lax.fori_loop(0, n_t, tile_body, (zero_row, zero_row))


def _s5_scan(p3, a_t, pw_t, b_t, cre_t, cim_t, d_skip):
    bsz, seq_len, _ = p3.shape
    cpb = W_BRANCH // LANES
    return pl.pallas_call(
        functools.partial(_s5_kernel, seq_len=seq_len),
        grid=(S5_LT, bsz),
        in_specs=[pl.BlockSpec((1, seq_len, LANES), lambda l, b: (b, 0, COL_D_IN * cpb + l)),
                  pl.BlockSpec((2, 2, 1, 1, S5_SW), lambda l, b: (0, 0, l, 0, 0)),
                  pl.BlockSpec((2, 2, 1, S5_SUB, S5_SW), lambda l, b: (0, 0, l, 0, 0)),
                  pl.BlockSpec((2, 1, LANES, 2 * S5_SW), lambda l, b: (0, l, 0, 0)),
                  pl.BlockSpec((2, 1, S5_SW, LANES), lambda l, b: (0, l, 0, 0)),
                  pl.BlockSpec((2, 1, S5_SW, LANES), lambda l, b: (0, l, 0, 0)),
                  pl.BlockSpec((1, LANES), lambda l, b: (0, l))],
        out_specs=pl.BlockSpec((1, seq_len, LANES), lambda l, b: (b, 0, l)),
        out_shape=jax.ShapeDtypeStruct((bsz, seq_len, W_BRANCH), F32),
        scratch_shapes=[pltpu.VMEM((S5_TILE, LANES), F32),
                        pltpu.VMEM((S5_TILE, S5_SW), F32),
                        pltpu.VMEM((S5_TILE, S5_SW), F32),
                        pltpu.VMEM((SUBLANES, S5_SW), F32),
                        pltpu.VMEM((SUBLANES, S5_SW), F32)],
        compiler_params=_cparams(("parallel", "parallel")),
        name="s5_scan",
    )(p3, a_t, pw_t, b_t, cre_t, cim_t, d_skip)


def _s5_post_kernel(y_ref, gate_ref, w_ref, b_ref, o_ref):
    y = y_ref[0]
    c0 = math.sqrt(2.0 / math.pi)
    z = 0.5 * y * (1.0 + jnp.tanh(c0 * (y + 0.044715 * (y * y * y))))
    lin = jnp.dot(z.astype(BF16), w_ref[...], preferred_element_type=F32) + b_ref[...]
    out = z * _sigmoid(lin) * _silu(gate_ref[0])
    o_ref[0] = out.astype(o_ref.dtype)


def _s5_post(y, p3, glu_w_bf, glu_b, tt=512):
    bsz, seq_len, _ = p3.shape
    blk = pl.BlockSpec((1, tt, W_BRANCH), lambda b, t: (b, t, 0))
    return pl.pallas_call(
        _s5_post_kernel,
        grid=(bsz, seq_len // tt),
        in_specs=[blk,
                  pl.BlockSpec((1, tt, W_BRANCH), lambda b, t: (b, t, COL_D_GATE)),
                  pl.BlockSpec((W_BRANCH, W_BRANCH), lambda b, t: (0, 0)),
                  pl.BlockSpec((1, W_BRANCH), lambda b, t: (0, 0))],
        out_specs=blk,
        out_shape=jax.ShapeDtypeStruct((bsz, seq_len, W_BRANCH), BF16),
        compiler_params=_cparams(("parallel", "parallel")),
        name="s5_post",
    )(y, p3, glu_w_bf, glu_b)


def _merge_kernel(ya_ref, yb_ref, yc_ref, yd_ref, r0_ref, r1_ref, r2_ref, r3_ref, w_ref, o_ref):
    acc = None
    for i, (y_ref, r_ref) in enumerate(((ya_ref, r0_ref), (yb_ref, r1_ref), (yc_ref, r2_ref), (yd_ref, r3_ref))):
        term = _sigmoid(r_ref[...]) * jnp.dot(y_ref[...], w_ref[i], preferred_element_type=F32)
        acc = term if acc is None else acc + term
    o_ref[...] = acc.astype(o_ref.dtype)


def _merge(ys, p2, w_stack_bf, tm=512, tn=512):
    t = p2.shape[0]
    npb = D_MODEL // tn
    roff = COL_R * W_BRANCH // tn
    yspec = pl.BlockSpec((tm, W_BRANCH), lambda j, i: (i, 0))

    def rspec(br):
        return pl.BlockSpec((tm, tn), lambda j, i: (i, roff + br * npb + j))

    return pl.pallas_call(
        _merge_kernel,
        grid=(npb, t // tm),
        in_specs=[yspec, yspec, yspec, yspec, rspec(0), rspec(1), rspec(2), rspec(3),
                  pl.BlockSpec((N_BRANCH, W_BRANCH, tn), lambda j, i: (0, 0, j))],
        out_specs=pl.BlockSpec((tm, tn), lambda j, i: (i, j)),
        out_shape=jax.ShapeDtypeStruct((t, D_MODEL), BF16),
        compiler_params=_cparams(("parallel", "parallel")),
        name="merge",
    )(*ys, p2, p2, p2, p2, w_stack_bf)


def _out_proj_kernel(x_ref, m_ref, w_ref, g_ref, o_ref, *, final_norm):
    y = x_ref[...] + jnp.dot(m_ref[...], w_ref[...], preferred_element_type=F32)
    if final_norm:
        y = y * lax.rsqrt(jnp.mean(y * y, axis=-1, keepdims=True) + EPS) * g_ref[...]
    o_ref[...] = y


def _out_proj(x2, m, w_bf, final_g, final_norm, tm=512):
    t = x2.shape[0]
    blk = pl.BlockSpec((tm, D_MODEL), lambda i: (i, 0))
    return pl.pallas_call(
        functools.partial(_out_proj_kernel, final_norm=final_norm),
        grid=(t // tm,),
        in_specs=[blk, blk,
                  pl.BlockSpec((D_MODEL, D_MODEL), lambda i: (0, 0)),
                  pl.BlockSpec((1, D_MODEL), lambda i: (0, 0))],
        out_specs=blk,
        out_shape=jax.ShapeDtypeStruct((t, D_MODEL), F32),
        compiler_params=_cparams(("parallel",)),
        name="out_proj",
    )(x2, m, w_bf, final_g)


def _lb_kernel(x_ref, o_ref):
    x = x_ref[...]
    e = jnp.exp(x - jnp.max(x, axis=0, keepdims=True))
    sm = e / jnp.sum(e, axis=0, keepdims=True)
    run = jnp.zeros_like(sm[0])
    for l in range(DEPTH):
        run = run + sm[l]
        o_ref[l] = run - sm[0]


def _lower_bounds(hg_lb):
    return pl.pallas_call(
        _lb_kernel,
        out_shape=jax.ShapeDtypeStruct(hg_lb.shape, F32),
        name="hg_lower_bounds",
    )(hg_lb)


def _layer(x, lw, final_g, final_norm):
    bsz, seq_len, _ = x.shape
    x2 = x.reshape(bsz * seq_len, D_MODEL)
    p2 = _in_proj(x2, lw["norm_g"], lw["w_in"])
    p3 = p2.reshape(bsz, seq_len, N_IN)
    ya = _conv_branch(p3, lw["conv_w"], lw["conv_b"], lw["conv_ln_g"], lw["conv_ln_b"])
    yb = _pool_branch(p3, lw["pool_w"], lw["pool_scale"])
    o_f, o_b = _hg_scan(p3, lw["lb"], lw["hg_nm"], lw["hg_mk"])
    yc = _hg_post(o_f, o_b, p3, lw["hg_norm_g"])
    y5 = _s5_scan(p3, *lw["s5"], lw["s5_d"])
    yd = _s5_post(y5, p3, lw["s5_glu_w"], lw["s5_glu_b"])
    ys = [y.reshape(bsz * seq_len, W_BRANCH) for y in (ya, yb, yc, yd)]
    m = _merge(ys, p2, lw["w_branch_out"])
    out = _out_proj(x2, m, lw["w_out"], final_g, final_norm)
    return out.reshape(bsz, seq_len, D_MODEL)


def kernel(x_prompt, x_sample, norm_g, w_in, conv_w, conv_b, conv_ln_g, conv_ln_b, w_a_out, pool_w, pool_scale, w_b_out, hg_lb, hg_norm_g, w_c_out, s5_a_re, s5_a_im, s5_log_dt, s5_b_re, s5_b_im, s5_c_re, s5_c_im, s5_d, s5_glu_w, s5_glu_b, w_d_out, w_out, final_g):
    lb_all = _lower_bounds(hg_lb)
    nm, mk = _hg_tables()
    nm_bf = jnp.asarray(nm, BF16)
    mk = jnp.asarray(mk, F32)
    layers = []
    for l in range(DEPTH):
        prep = _s5_prep(s5_a_re[l], s5_a_im[l], s5_log_dt[l], s5_b_re[l], s5_b_im[l])
        layers.append(dict(
            norm_g=norm_g[l][None], w_in=w_in[l].astype(BF16),
            conv_w=conv_w[l], conv_b=conv_b[l][None], conv_ln_g=conv_ln_g[l][None], conv_ln_b=conv_ln_b[l][None],
            pool_w=pool_w[l].astype(BF16), pool_scale=pool_scale[l][None],
            lb=lb_all[l], hg_nm=nm_bf, hg_mk=mk, hg_norm_g=hg_norm_g[l][None],
            s5=_s5_layout(*prep, s5_c_re[l], s5_c_im[l]), s5_d=s5_d[l][None],
            s5_glu_w=s5_glu_w[l].astype(BF16), s5_glu_b=s5_glu_b[l][None],
            w_branch_out=jnp.stack([w_a_out[l], w_b_out[l], w_c_out[l], w_d_out[l]]).astype(BF16),
            w_out=w_out[l].astype(BF16)))
    fg = final_g[None]

    def run(x):
        for l in range(DEPTH):
            x = _layer(x, layers[l], fg, l == DEPTH - 1)
        return x

    return (run(x_prompt), run(x_sample))
```

```python
import functools
import math

import numpy as np
import jax
import jax.numpy as jnp
from jax import lax
from jax.experimental import pallas as pl
from jax.experimental.pallas import tpu as pltpu

F32 = jnp.float32
BF16 = jnp.bfloat16

D_MODEL = 2048
DEPTH = 2
W_BRANCH = 1024
N_BRANCH = 4
N_IN = 12 * W_BRANCH + N_BRANCH * D_MODEL
CONV_WIDTH = 31
CONV_PAD = CONV_WIDTH // 2
POOL_WINDOWS = (2, 4, 8, 16)
POOL_GROUP = W_BRANCH // len(POOL_WINDOWS)
HG_HEAD_DIM = 128
HG_HEADS = W_BRANCH // HG_HEAD_DIM
S5_GROUP_CH = 16
S5_GROUPS = W_BRANCH // S5_GROUP_CH
S5_STATE = 64
EPS = 1e-6

LANES = 128
SUBLANES = 8
VMEM_LIMIT = 52 * 1024 * 1024

COL_A_VAL, COL_A_GLU, COL_A_GATE, COL_B_IN, COL_B_GATE = 0, 1, 2, 3, 4
COL_C_Q, COL_C_FF, COL_C_FB, COL_C_I, COL_C_GATE, COL_D_IN, COL_D_GATE = 5, 6, 7, 8, 9, 10, 11
COL_R = 12

HALO = 16
CONV_LANES = 256
HG_TILE = 256
HG_HALF = HG_TILE // 2
HG_LEVELS = 8
HG_NROWS = 4 * HG_TILE
S5_TILE = 512
S5_SUB = S5_TILE // SUBLANES
S5_LT = W_BRANCH // LANES
S5_GPT = LANES // S5_GROUP_CH
S5_SW = S5_GPT * S5_STATE


def _sigmoid(x):
    return 1.0 / (1.0 + jnp.exp(-x))


def _silu(x):
    return x * _sigmoid(x)


def _cparams(sem):
    return pltpu.CompilerParams(dimension_semantics=sem, vmem_limit_bytes=VMEM_LIMIT)


def _in_proj_kernel(x_ref, g_ref, w_ref, o_ref, h_ref):
    @pl.when(pl.program_id(1) == 0)
    def _():
        x = x_ref[...]
        ms = jnp.mean(x * x, axis=-1, keepdims=True)
        h_ref[...] = (x * lax.rsqrt(ms + EPS) * g_ref[...]).astype(BF16)

    o_ref[...] = jnp.dot(h_ref[...], w_ref[...], preferred_element_type=F32)


def _in_proj(x2, g, w_bf, tm=1024, tn=1024):
    t = x2.shape[0]
    return pl.pallas_call(
        _in_proj_kernel,
        grid=(t // tm, N_IN // tn),
        in_specs=[pl.BlockSpec((tm, D_MODEL), lambda i, j: (i, 0)),
                  pl.BlockSpec((1, D_MODEL), lambda i, j: (0, 0)),
                  pl.BlockSpec((D_MODEL, tn), lambda i, j: (0, j))],
        out_specs=pl.BlockSpec((tm, tn), lambda i, j: (i, j)),
        out_shape=jax.ShapeDtypeStruct((t, N_IN), F32),
        scratch_shapes=[pltpu.VMEM((tm, D_MODEL), BF16)],
        compiler_params=_cparams(("parallel", "arbitrary")),
        name="in_proj",
    )(x2, g, w_bf)


def _conv_kernel(main_ref, gate_ref, prev_ref, next_ref, cw_ref, cb_ref, lg_ref, lb_ref, o_ref,
                 us_scr, wb_scr, *, tt, rc):
    t = pl.program_id(1)
    n_t = pl.num_programs(1)

    def glu(ref):
        return ref[0, :, :W_BRANCH] * _sigmoid(ref[0, :, W_BRANCH:])

    us_scr[0, HALO:HALO + tt, :] = glu(main_ref)
    us_scr[0, 0:HALO, :] = jnp.where(t > 0, glu(prev_ref), 0.0)
    us_scr[0, HALO + tt:2 * HALO + tt, :] = jnp.where(t < n_t - 1, glu(next_ref), 0.0)
    n = tt + 2 * HALO - SUBLANES
    for s in range(1, SUBLANES):
        us_scr[s, 0:n, :] = us_scr[0, s:s + n, :]
    for j in range(CONV_WIDTH):
        wb_scr[j] = jnp.broadcast_to(cw_ref[j:j + 1, :], (SUBLANES, W_BRANCH))

    def body(c, carry):
        r0 = pl.multiple_of(c * rc, rc)
        parts = []
        for lc in range(W_BRANCH // CONV_LANES):
            lanes = slice(lc * CONV_LANES, (lc + 1) * CONV_LANES)
            acc = jnp.zeros((rc // SUBLANES, SUBLANES, CONV_LANES), F32)
            for j in range(CONV_WIDTH):
                off = HALO - CONV_PAD + j
                s = off % SUBLANES
                tap = us_scr[s, pl.ds(r0 + (off - s), rc), lanes]
                acc = acc + tap.reshape(rc // SUBLANES, SUBLANES, CONV_LANES) * wb_scr[j, :, lanes]
            parts.append(acc.reshape(rc, CONV_LANES))
        y = jnp.concatenate(parts, axis=-1) + cb_ref[...]
        mu = jnp.mean(y, axis=-1, keepdims=True)
        yc = y - mu
        var = jnp.mean(yc * yc, axis=-1, keepdims=True)
        yn = yc * lax.rsqrt(var + EPS) * lg_ref[...] + lb_ref[...]
        out = _silu(yn) * _silu(gate_ref[0, pl.ds(r0, rc), :])
        o_ref[0, pl.ds(r0, rc), :] = out.astype(o_ref.dtype)
        return carry

    lax.fori_loop(0, tt // rc, body, 0)


def _halo_specs(tt, width, col_blk, seq_len):
    per = tt // HALO
    last = seq_len // HALO - 1
    prev = pl.BlockSpec((1, HALO, width), lambda b, t: (b, jnp.maximum(t * per - 1, 0), col_blk))
    nxt = pl.BlockSpec((1, HALO, width), lambda b, t: (b, jnp.minimum((t + 1) * per, last), col_blk))
    return prev, nxt


def _conv_branch(p3, cw, cb, lg, lb, tt=512, rc=32):
    bsz, seq_len, _ = p3.shape
    prev, nxt = _halo_specs(tt, 2 * W_BRANCH, 0, seq_len)
    vec = pl.BlockSpec((1, W_BRANCH), lambda b, t: (0, 0))
    return pl.pallas_call(
        functools.partial(_conv_kernel, tt=tt, rc=rc),
        grid=(bsz, seq_len // tt),
        in_specs=[pl.BlockSpec((1, tt, 2 * W_BRANCH), lambda b, t: (b, t, 0)),
                  pl.BlockSpec((1, tt, W_BRANCH), lambda b, t: (b, t, COL_A_GATE)),
                  prev, nxt,
                  pl.BlockSpec((CONV_WIDTH, W_BRANCH), lambda b, t: (0, 0)),
                  vec, vec, vec],
        out_specs=pl.BlockSpec((1, tt, W_BRANCH), lambda b, t: (b, t, 0)),
        out_shape=jax.ShapeDtypeStruct((bsz, seq_len, W_BRANCH), BF16),
        scratch_shapes=[pltpu.VMEM((SUBLANES, tt + 2 * HALO, W_BRANCH), F32),
                        pltpu.VMEM((CONV_WIDTH, SUBLANES, W_BRANCH), F32)],
        compiler_params=_cparams(("parallel", "parallel")),
        name="conv_branch",
    )(p3, p3, p3, p3, cw, cb, lg, lb)


def _pool_kernel(main_ref, gate_ref, prev_ref, next_ref, pw_ref, ps_ref, o_ref, e_scr, s_scr,
                 *, tt, seq_len):
    t = pl.program_id(1)
    n_t = pl.num_programs(1)
    rows = tt + 2 * HALO
    e_scr[HALO:HALO + tt, :] = main_ref[0]
    e_scr[0:HALO, :] = jnp.where(t > 0, prev_ref[0], 0.0)
    e_scr[HALO + tt:rows, :] = jnp.where(t < n_t - 1, next_ref[0], 0.0)

    tpos = t * tt + lax.broadcasted_iota(jnp.int32, (tt, 1), 0)
    for g, win in enumerate(POOL_WINDOWS):
        lanes = slice(g * POOL_GROUP, (g + 1) * POOL_GROUP)
        n = rows - 2
        s_scr[1:1 + n, lanes] = e_scr[1:1 + n, lanes] + e_scr[0:n, lanes]
        half = 1
        while 2 * half < win:
            n = rows - 2 * half - 2 * half
            lo = 2 * half
            a = s_scr[lo + half:lo + half + n, lanes]
            b = s_scr[lo - half:lo - half + n, lanes]
            s_scr[lo:lo + n, lanes] = a + b
            half *= 2
        left = win // 2
        right = win - 1 - left
        cnt = (jnp.minimum(tpos + right, seq_len - 1) + 1 - jnp.maximum(tpos - left, 0)).astype(F32)
        u = e_scr[HALO:HALO + tt, lanes]
        d = s_scr[HALO:HALO + tt, lanes] / cnt - u
        y = jnp.dot(d.astype(BF16), pw_ref[g], preferred_element_type=F32)
        y = y * ps_ref[:, lanes] * _silu(gate_ref[0, :, lanes])
        o_ref[0, :, lanes] = y.astype(o_ref.dtype)


def _pool_branch(p3, pw_bf, ps, tt=512):
    bsz, seq_len, _ = p3.shape
    prev, nxt = _halo_specs(tt, W_BRANCH, COL_B_IN, seq_len)
    return pl.pallas_call(
        functools.partial(_pool_kernel, tt=tt, seq_len=seq_len),
        grid=(bsz, seq_len // tt),
        in_specs=[pl.BlockSpec((1, tt, W_BRANCH), lambda b, t: (b, t, COL_B_IN)),
                  pl.BlockSpec((1, tt, W_BRANCH), lambda b, t: (b, t, COL_B_GATE)),
                  prev, nxt,
                  pl.BlockSpec((len(POOL_WINDOWS), POOL_GROUP, POOL_GROUP), lambda b, t: (0, 0, 0)),
                  pl.BlockSpec((1, W_BRANCH), lambda b, t: (0, 0))],
        out_specs=pl.BlockSpec((1, tt, W_BRANCH), lambda b, t: (b, t, 0)),
        out_shape=jax.ShapeDtypeStruct((bsz, seq_len, W_BRANCH), BF16),
        scratch_shapes=[pltpu.VMEM((tt + 2 * HALO, W_BRANCH), F32),
                        pltpu.VMEM((tt + 2 * HALO, W_BRANCH), F32)],
        compiler_params=_cparams(("parallel", "parallel")),
        name="pool_branch",
    )(p3, p3, p3, p3, pw_bf, ps)


def _hg_tables():
    c = HG_TILE
    t = np.arange(c)
    rr = np.arange(c)[None, :]
    vstart = (t // SUBLANES) * SUBLANES
    pref = [t, vstart + 7, vstart + 3, vstart + np.where(t % SUBLANES < 4, 1, 5)]
    nmat = np.stack([rr <= p[:, None] for p in pref]).astype(np.float32)
    masks = np.zeros((HG_LEVELS, HG_HALF, HG_HALF), np.float32)
    masks[0] = np.eye(HG_HALF)
    th = np.arange(HG_HALF)
    for lvl in range(HG_LEVELS - 1):
        half = 1 << lvl
        bh = th // (2 * half)
        lh = (th % (2 * half)) >= half
        masks[1 + lvl] = (bh[:, None] == bh[None, :]) & lh[:, None] & (~lh)[None, :]
    nm = np.stack([nmat, nmat[:, ::-1, ::-1]]).reshape(2, HG_NROWS, c)
    mk = np.stack([masks, masks[:, ::-1, ::-1]])
    return nm, mk


def _dot_nt(a, b):
    return lax.dot_general(a, b, (((1,), (1,)), ((), ())), preferred_element_type=F32)


def _hg_ref_rows(lvl, d):
    out = []
    for m in range(HG_TILE // SUBLANES):
        block = 2 << lvl
        mid = (m * SUBLANES // block) * block + block // 2
        out.append(mid // SUBLANES - 1 if d == 0 else mid // SUBLANES)
    return out


def _hg_gates(q, z, v, lbv, nm_ref, d):
    c = HG_TILE
    f = lbv + (1.0 - lbv) * _sigmoid(z)
    lf = jnp.log2(f)
    kk = (1.0 - lbv) * _sigmoid(-z)
    hi = lf.astype(BF16)
    lo = (lf - hi.astype(F32)).astype(BF16)
    tab = jnp.dot(nm_ref[d], jnp.concatenate([hi, lo], axis=1), preferred_element_type=F32)
    tab = tab[:, :HG_HEAD_DIM] + tab[:, HG_HEAD_DIM:]
    b, b_grp, b_l2, b_l1 = (tab[i * c:(i + 1) * c] for i in range(4))
    b_end = b_grp[c - 1:c] if d == 0 else b_grp[0:1]
    n_g = c // SUBLANES
    sub = lax.broadcasted_iota(jnp.int32, (1, SUBLANES, HG_HEAD_DIM), 1)
    q3 = q.reshape(n_g, SUBLANES, HG_HEAD_DIM)
    k3 = kk.reshape(n_g, SUBLANES, HG_HEAD_DIM)
    xs = []
    for lvl in range(HG_LEVELS):
        if (1 << lvl) < SUBLANES:
            later = (jnp.right_shift(sub, lvl) & 1) == (1 - d)
            base = jnp.where(later, q3, k3).reshape(c, HG_HEAD_DIM)
        else:
            pick = [((m * SUBLANES) >> lvl) & 1 == (1 - d) for m in range(n_g)]
            base = jnp.concatenate([(q3 if p else k3)[m] for m, p in enumerate(pick)], axis=0)
        if lvl == 0:
            later0 = jnp.broadcast_to(later, (n_g, SUBLANES, HG_HEAD_DIM)).reshape(c, HG_HEAD_DIM)
            xs.append(jnp.where(later0, base * f, base))
            continue
        if lvl == 1:
            ref = b_l1
        elif lvl == 2:
            ref = b_l2
        else:
            ref = jnp.concatenate([b_grp[g * SUBLANES:(g + 1) * SUBLANES] for g in _hg_ref_rows(lvl, d)], axis=0)
        neg_abs = pltpu.bitcast(pltpu.bitcast(b - ref, jnp.uint32) | jnp.uint32(0x80000000), F32)
        xs.append(base * jnp.exp2(neg_abs))
    return dict(q=q, kk=kk, v=v, xs=xs, q_in=q * jnp.exp2(b), k_out=kk * jnp.exp2(b_end - b),
                bend=jnp.exp2(b_end))


def _hg_kernel(qf_ref, zf_ref, vf_ref, qb_ref, zb_ref, vb_ref, lb_ref, nm_ref, mk_ref,
               of_ref, ob_ref, sf_scr, sb_scr):
    @pl.when(pl.program_id(2) == 0)
    def _():
        sf_scr[...] = jnp.zeros_like(sf_scr)
        sb_scr[...] = jnp.zeros_like(sb_scr)

    c = HG_TILE
    halves = (slice(0, HG_HALF), slice(HG_HALF, c))
    ins = ((qf_ref, zf_ref, vf_ref), (qb_ref, zb_ref, vb_ref))
    g = [_hg_gates(ins[d][0][0], ins[d][1][0], ins[d][2][0], lb_ref[d:d + 1, :], nm_ref, d) for d in range(2)]
    scores = [[[], []] for _ in range(2)]
    for d in range(2):
        qb = g[d]["q"].astype(BF16)
        kb = g[d]["kk"].astype(BF16)
        for i, rs in enumerate(halves):
            scores[d][i].append((_dot_nt(qb[rs], kb[rs]), 0))
        for lvl in range(HG_LEVELS - 1):
            x = g[d]["xs"][lvl].astype(BF16)
            for i, rs in enumerate(halves):
                scores[d][i].append((_dot_nt(x[rs], x[rs]), 1 + lvl))
        x = g[d]["xs"][HG_LEVELS - 1].astype(BF16)
        late, early = (1, 0) if d == 0 else (0, 1)
        scores[d][late].append((_dot_nt(x[halves[late]], x[halves[early]]), None))
    outs = []
    for d in range(2):
        vb = g[d]["v"].astype(BF16)
        late, early = (1, 0) if d == 0 else (0, 1)
        acc = []
        for i, rs in enumerate(halves):
            sm = [s.astype(BF16) * mk_ref[d, m] if m is not None else s.astype(BF16) for s, m in scores[d][i]]
            vs = [vb[rs]] * HG_LEVELS + ([vb[halves[early]]] if i == late else [])
            acc.append(jnp.dot(jnp.concatenate(sm, axis=1), jnp.concatenate(vs, axis=0),
                               preferred_element_type=F32))
        outs.append(acc)
    st_refs = (sf_scr, sb_scr)
    o_refs = (of_ref, ob_ref)
    for d in range(2):
        st = st_refs[d][...]
        o_refs[d][0] = (jnp.concatenate(outs[d], axis=0)
                        + _dot_nt(g[d]["q_in"].astype(BF16), st.astype(BF16)))
        vt = jnp.transpose(g[d]["v"]).astype(BF16)
        st_refs[d][...] = st * g[d]["bend"] + jnp.dot(vt, g[d]["k_out"].astype(BF16),
                                                      preferred_element_type=F32)


def _hg_scan(p3, lb, nm_bf, mk_bf):
    bsz, seq_len, _ = p3.shape
    tt = HG_TILE
    n_t = seq_len // tt
    cpb = W_BRANCH // HG_HEAD_DIM

    def fspec(col):
        return pl.BlockSpec((1, tt, HG_HEAD_DIM), lambda b, h, k: (b, k, col * cpb + h))

    def bspec(col):
        return pl.BlockSpec((1, tt, HG_HEAD_DIM), lambda b, h, k: (b, n_t - 1 - k, col * cpb + h))

    out = jax.ShapeDtypeStruct((bsz, seq_len, W_BRANCH), F32)
    return pl.pallas_call(
        _hg_kernel,
        grid=(bsz, HG_HEADS, n_t),
        in_specs=[fspec(COL_C_Q), fspec(COL_C_FF), fspec(COL_C_I),
                  bspec(COL_C_Q), bspec(COL_C_FB), bspec(COL_C_I),
                  pl.BlockSpec((2, HG_HEAD_DIM), lambda b, h, k: (0, h)),
                  pl.BlockSpec((2, HG_NROWS, HG_TILE), lambda b, h, k: (0, 0, 0)),
                  pl.BlockSpec((2, HG_LEVELS, HG_HALF, HG_HALF), lambda b, h, k: (0, 0, 0, 0))],
        out_specs=[pl.BlockSpec((1, tt, HG_HEAD_DIM), lambda b, h, k: (b, k, h)),
                   pl.BlockSpec((1, tt, HG_HEAD_DIM), lambda b, h, k: (b, n_t - 1 - k, h))],
        out_shape=[out, out],
        scratch_shapes=[pltpu.VMEM((HG_HEAD_DIM, HG_HEAD_DIM), F32),
                        pltpu.VMEM((HG_HEAD_DIM, HG_HEAD_DIM), F32)],
        compiler_params=_cparams(("parallel", "parallel", "arbitrary")),
        name="hgrn2_scan",
    )(p3, p3, p3, p3, p3, p3, lb, nm_bf, mk_bf)


def _hg_post_kernel(of_ref, ob_ref, gate_ref, g_ref, o_ref):
    for h in range(HG_HEADS):
        lanes = slice(h * HG_HEAD_DIM, (h + 1) * HG_HEAD_DIM)
        o = of_ref[0, :, lanes] + ob_ref[0, :, lanes]
        o = o * lax.rsqrt(jnp.mean(o * o, axis=-1, keepdims=True) + EPS)
        y = o * g_ref[:, lanes] * _silu(gate_ref[0, :, lanes])
        o_ref[0, :, lanes] = y.astype(o_ref.dtype)


def _hg_post(o_f, o_b, p3, norm_g, tt=512):
    bsz, seq_len, _ = p3.shape
    blk = pl.BlockSpec((1, tt, W_BRANCH), lambda b, t: (b, t, 0))
    return pl.pallas_call(
        _hg_post_kernel,
        grid=(bsz, seq_len // tt),
        in_specs=[blk, blk,
                  pl.BlockSpec((1, tt, W_BRANCH), lambda b, t: (b, t, COL_C_GATE)),
                  pl.BlockSpec((1, W_BRANCH), lambda b, t: (0, 0))],
        out_specs=blk,
        out_shape=jax.ShapeDtypeStruct((bsz, seq_len, W_BRANCH), BF16),
        compiler_params=_cparams(("parallel", "parallel")),
        name="hgrn2_post",
    )(o_f, o_b, p3, norm_g)


def _s5_prep_kernel(are_ref, aim_ref, ldt_ref, bre_ref, bim_ref,
                    abr_ref, abi_ref, bbr_ref, bbi_ref, pwr_ref, pwi_ref):
    a_re = are_ref[...]
    a_im = aim_ref[...]
    dt = jnp.exp(ldt_ref[...])
    mag = jnp.exp(dt * a_re)
    ang = dt * a_im
    ab_re = mag * jnp.cos(ang)
    ab_im = mag * jnp.sin(ang)
    den = a_re * a_re + a_im * a_im
    x_ = ab_re - 1.0
    y_ = ab_im
    g_re = (x_ * a_re + y_ * a_im) / den
    g_im = (y_ * a_re - x_ * a_im) / den
    abr_ref[...] = ab_re
    abi_ref[...] = ab_im
    for c in range(S5_GROUP_CH):
        b_re = bre_ref[c]
        b_im = bim_ref[c]
        bbr_ref[c] = g_re * b_re - g_im * b_im
        bbi_ref[c] = g_re * b_im + g_im * b_re
    p_re = ab_re
    p_im = ab_im
    for i in range(S5_SUB):
        pwr_ref[i] = p_re
        pwi_ref[i] = p_im
        p_re, p_im = p_re * ab_re - p_im * ab_im, p_re * ab_im + p_im * ab_re


def _s5_prep(a_re, a_im, log_dt, b_re, b_im):
    n = 2 * S5_GROUPS
    sp = jax.ShapeDtypeStruct((n, S5_STATE), F32)
    sb = jax.ShapeDtypeStruct((S5_GROUP_CH, n, S5_STATE), F32)
    spw = jax.ShapeDtypeStruct((S5_SUB, n, S5_STATE), F32)
    return pl.pallas_call(
        _s5_prep_kernel,
        out_shape=[sp, sp, sb, sb, spw, spw],
        name="s5_prep",
    )(a_re.reshape(n, S5_STATE), a_im.reshape(n, S5_STATE), log_dt.reshape(n, 1),
      jnp.transpose(b_re.reshape(n, S5_STATE, S5_GROUP_CH), (2, 0, 1)),
      jnp.transpose(b_im.reshape(n, S5_STATE, S5_GROUP_CH), (2, 0, 1)))


def _s5_layout(abr, abi, bbr, bbi, pwr, pwi, c_re, c_im):
    eye = jnp.eye(S5_GPT, dtype=F32)

    def tiles(x):
        return x.reshape(2, S5_LT, 1, S5_SW)

    a_t = jnp.stack([tiles(abr), tiles(abi)], axis=1)

    def pw_tiles(x):
        return jnp.transpose(x.reshape(S5_SUB, 2, S5_LT, S5_SW), (1, 2, 0, 3))

    pw_t = jnp.stack([pw_tiles(pwr), pw_tiles(pwi)], axis=1)

    def b_bd(x):
        x = jnp.transpose(x, (1, 0, 2)).reshape(2, S5_LT, S5_GPT, S5_GROUP_CH, S5_STATE)
        bd = x[:, :, :, :, None, :] * eye[None, None, :, None, :, None]
        return bd.reshape(2, S5_LT, LANES, S5_SW)

    b_t = jnp.concatenate([b_bd(bbr), b_bd(bbi)], axis=-1).astype(BF16)

    def c_bd(x):
        x = jnp.transpose(x.reshape(2, S5_LT, S5_GPT, S5_GROUP_CH, S5_STATE), (0, 1, 2, 4, 3))
        bd = x[:, :, :, :, None, :] * eye[None, None, :, None, :, None]
        return bd.reshape(2, S5_LT, S5_SW, LANES).astype(BF16)

    return a_t, pw_t, b_t, c_bd(c_re), c_bd(-c_im)


def _s5_perm():
    n = SUBLANES * SUBLANES
    r = np.arange(n)
    p = np.zeros((n, n), np.float32)
    p[r, (r % SUBLANES) * SUBLANES + r // SUBLANES] = 1.0
    return p


def _s5_to_wide(x, swap):
    g = SUBLANES

    def grp(p, q):
        k = (q * g + p) if swap else (p * g + q)
        return x[k * g:(k + 1) * g, :]

    return jnp.concatenate([jnp.concatenate([grp(p, q) for q in range(g)], axis=1) for p in range(g)], axis=0)


def _s5_from_wide(x, swap):
    g = SUBLANES
    n = x.shape[1] // g
    order = [(k % g, k // g) if swap else (k // g, k % g) for k in range(g * g)]
    return jnp.concatenate([x[p * g:(p + 1) * g, q * n:(q + 1) * n] for p, q in order], axis=0)


def _s5_kernel(u_ref, a_ref, pw_ref, b_ref, cre_ref, cimn_ref, d_ref, pm_ref, o_ref,
               hre_scr, him_scr, hbr_scr, hbi_scr, hin_re_scr, hin_im_scr, pwb_scr, *, seq_len):
    n_t = seq_len // S5_TILE
    o_ref[0] = u_ref[0] * d_ref[...]
    for d in range(2):
        for c in range(2):
            for i in range(S5_SUB):
                pwb_scr[d, c, i] = jnp.broadcast_to(pw_ref[d, c, 0, i:i + 1, :], (SUBLANES, S5_SW))
    zero_row = jnp.zeros((1, S5_SW), F32)
    zero_blk = jnp.zeros((SUBLANES, S5_SW), F32)
    ar = [jnp.broadcast_to(a_ref[d, 0, 0], (SUBLANES, S5_SW)) for d in range(2)]
    ai = [jnp.broadcast_to(a_ref[d, 1, 0], (SUBLANES, S5_SW)) for d in range(2)]
    asr = [pw_ref[d, 0, 0, S5_SUB - 1:S5_SUB, :] for d in range(2)]
    asi = [pw_ref[d, 1, 0, S5_SUB - 1:S5_SUB, :] for d in range(2)]

    def blk(i):
        return pl.ds(pl.multiple_of(i * SUBLANES, SUBLANES), SUBLANES)

    def tile_body(k, carry):
        t0 = [pl.multiple_of(k * S5_TILE, S5_TILE), pl.multiple_of((n_t - 1 - k) * S5_TILE, S5_TILE)]
        us = [_s5_to_wide(u_ref[0, pl.ds(t0[d], S5_TILE), :].astype(BF16), False) for d in range(2)]
        ups = [jnp.dot(pm_ref[...], us[d], preferred_element_type=F32).astype(BF16) for d in range(2)]
        bus = [jnp.dot(_s5_from_wide(ups[d], True), b_ref[d, 0], preferred_element_type=F32) for d in range(2)]
        for d in range(2):
            hre_scr[d] = bus[d][:, :S5_SW]
            him_scr[d] = bus[d][:, S5_SW:]

        def step(i, h):
            out = []
            for d in range(2):
                rows = blk(i if d == 0 else S5_SUB - 1 - i)
                hr, hi = h[2 * d], h[2 * d + 1]
                nr = ar[d] * hr - ai[d] * hi + hre_scr[d, rows, :]
                ni = ar[d] * hi + ai[d] * hr + him_scr[d, rows, :]
                hre_scr[d, rows, :] = nr
                him_scr[d, rows, :] = ni
                out += [nr, ni]
            return tuple(out)

        ends = lax.fori_loop(0, S5_SUB, step, (zero_blk,) * 4, unroll=2)
        new_carry = []
        for d in range(2):
            c_re, c_im = carry[2 * d], carry[2 * d + 1]
            for j in (range(SUBLANES) if d == 0 else range(SUBLANES - 1, -1, -1)):
                hin_re_scr[d, j:j + 1, :] = c_re
                hin_im_scr[d, j:j + 1, :] = c_im
                er = ends[2 * d][j:j + 1, :]
                ei = ends[2 * d + 1][j:j + 1, :]
                c_re, c_im = er + asr[d] * c_re - asi[d] * c_im, ei + asr[d] * c_im + asi[d] * c_re
            new_carry += [c_re, c_im]
        hin_re = [hin_re_scr[d] for d in range(2)]
        hin_im = [hin_im_scr[d] for d in range(2)]

        def fix(i2, _):
            for d in range(2):
                res_re, res_im = [], []
                for half in range(2):
                    i = 2 * i2 + half
                    rows = blk(i)
                    pi = i if d == 0 else S5_SUB - 1 - i
                    pr = pwb_scr[d, 0, pi]
                    pim = pwb_scr[d, 1, pi]
                    res_re.append(hre_scr[d, rows, :] + pr * hin_re[d] - pim * hin_im[d])
                    res_im.append(him_scr[d, rows, :] + pr * hin_im[d] + pim * hin_re[d])
                rows2 = pl.ds(pl.multiple_of(i2 * 2 * SUBLANES, 2 * SUBLANES), 2 * SUBLANES)
                hbr_scr[d, rows2, :] = jnp.concatenate(res_re, axis=0).astype(BF16)
                hbi_scr[d, rows2, :] = jnp.concatenate(res_im, axis=0).astype(BF16)
            return 0

        lax.fori_loop(0, S5_SUB // 2, fix, 0)
        ys = [jnp.dot(hbr_scr[d], cre_ref[d, 0], preferred_element_type=F32)
              + jnp.dot(hbi_scr[d], cimn_ref[d, 0], preferred_element_type=F32) for d in range(2)]
        ycs = []
        for d in range(2):
            y_hi = ys[d].astype(BF16)
            y_lo = (ys[d] - y_hi.astype(F32)).astype(BF16)
            ycs.append(jnp.concatenate([_s5_to_wide(y_hi, True), _s5_to_wide(y_lo, True)], axis=1))
        yns = [jnp.dot(pm_ref[...], ycs[d], preferred_element_type=F32) for d in range(2)]
        for d in range(2):
            yn = _s5_from_wide(yns[d][:, :SUBLANES * LANES] + yns[d][:, SUBLANES * LANES:], False)
            rows = pl.ds(t0[d], S5_TILE)
            o_ref[0, rows, :] = o_ref[0, rows, :] + yn
        return tuple(new_carry)

    lax.fori_loop(0, n_t, tile_body, (zero_row,) * 4)


def _s5_scan(p3, a_t, pw_t, b_t, cre_t, cimn_t, d_skip, perm):
    bsz, seq_len, _ = p3.shape
    cpb = W_BRANCH // LANES
    return pl.pallas_call(
        functools.partial(_s5_kernel, seq_len=seq_len),
        grid=(S5_LT, bsz),
        in_specs=[pl.BlockSpec((1, seq_len, LANES), lambda l, b: (b, 0, COL_D_IN * cpb + l)),
                  pl.BlockSpec((2, 2, 1, 1, S5_SW), lambda l, b: (0, 0, l, 0, 0)),
                  pl.BlockSpec((2, 2, 1, S5_SUB, S5_SW), lambda l, b: (0, 0, l, 0, 0)),
                  pl.BlockSpec((2, 1, LANES, 2 * S5_SW), lambda l, b: (0, l, 0, 0)),
                  pl.BlockSpec((2, 1, S5_SW, LANES), lambda l, b: (0, l, 0, 0)),
                  pl.BlockSpec((2, 1, S5_SW, LANES), lambda l, b: (0, l, 0, 0)),
                  pl.BlockSpec((1, LANES), lambda l, b: (0, l)),
                  pl.BlockSpec((SUBLANES * SUBLANES, SUBLANES * SUBLANES), lambda l, b: (0, 0))],
        out_specs=pl.BlockSpec((1, seq_len, LANES), lambda l, b: (b, 0, l)),
        out_shape=jax.ShapeDtypeStruct((bsz, seq_len, W_BRANCH), F32),
        scratch_shapes=[pltpu.VMEM((2, S5_TILE, S5_SW), F32),
                        pltpu.VMEM((2, S5_TILE, S5_SW), F32),
                        pltpu.VMEM((2, S5_TILE, S5_SW), BF16),
                        pltpu.VMEM((2, S5_TILE, S5_SW), BF16),
                        pltpu.VMEM((2, SUBLANES, S5_SW), F32),
                        pltpu.VMEM((2, SUBLANES, S5_SW), F32),
                        pltpu.VMEM((2, 2, S5_SUB, SUBLANES, S5_SW), F32)],
        compiler_params=_cparams(("parallel", "parallel")),
        name="s5_scan",
    )(p3, a_t, pw_t, b_t, cre_t, cimn_t, d_skip, perm)


def _s5_post_kernel(y_ref, gate_ref, w_ref, b_ref, o_ref):
    y = y_ref[0]
    c0 = math.sqrt(2.0 / math.pi)
    z = 0.5 * y * (1.0 + jnp.tanh(c0 * (y + 0.044715 * (y * y * y))))
    lin = jnp.dot(z.astype(BF16), w_ref[...], preferred_element_type=F32) + b_ref[...]
    out = z * _sigmoid(lin) * _silu(gate_ref[0])
    o_ref[0] = out.astype(o_ref.dtype)


def _s5_post(y, p3, glu_w_bf, glu_b, tt=512):
    bsz, seq_len, _ = p3.shape
    blk = pl.BlockSpec((1, tt, W_BRANCH), lambda b, t: (b, t, 0))
    return pl.pallas_call(
        _s5_post_kernel,
        grid=(bsz, seq_len // tt),
        in_specs=[blk,
                  pl.BlockSpec((1, tt, W_BRANCH), lambda b, t: (b, t, COL_D_GATE)),
                  pl.BlockSpec((W_BRANCH, W_BRANCH), lambda b, t: (0, 0)),
                  pl.BlockSpec((1, W_BRANCH), lambda b, t: (0, 0))],
        out_specs=blk,
        out_shape=jax.ShapeDtypeStruct((bsz, seq_len, W_BRANCH), BF16),
        compiler_params=_cparams(("parallel", "parallel")),
        name="s5_post",
    )(y, p3, glu_w_bf, glu_b)


def _merge_kernel(ya_ref, yb_ref, yc_ref, yd_ref, r0_ref, r1_ref, r2_ref, r3_ref, w_ref, o_ref):
    acc = None
    for i, (y_ref, r_ref) in enumerate(((ya_ref, r0_ref), (yb_ref, r1_ref), (yc_ref, r2_ref), (yd_ref, r3_ref))):
        term = _sigmoid(r_ref[...]) * jnp.dot(y_ref[...], w_ref[i], preferred_element_type=F32)
        acc = term if acc is None else acc + term
    o_ref[...] = acc.astype(o_ref.dtype)


def _merge(ys, p2, w_stack_bf, tm=512, tn=1024):
    t = p2.shape[0]
    npb = D_MODEL // tn
    roff = COL_R * W_BRANCH // tn
    yspec = pl.BlockSpec((tm, W_BRANCH), lambda j, i: (i, 0))

    def rspec(br):
        return pl.BlockSpec((tm, tn), lambda j, i: (i, roff + br * npb + j))

    return pl.pallas_call(
        _merge_kernel,
        grid=(npb, t // tm),
        in_specs=[yspec, yspec, yspec, yspec, rspec(0), rspec(1), rspec(2), rspec(3),
                  pl.BlockSpec((N_BRANCH, W_BRANCH, tn), lambda j, i: (0, 0, j))],
        out_specs=pl.BlockSpec((tm, tn), lambda j, i: (i, j)),
        out_shape=jax.ShapeDtypeStruct((t, D_MODEL), BF16),
        compiler_params=_cparams(("parallel", "parallel")),
        name="merge",
    )(*ys, p2, p2, p2, p2, w_stack_bf)


def _out_proj_kernel(x_ref, m_ref, w_ref, g_ref, o_ref, *, final_norm):
    y = x_ref[...] + jnp.dot(m_ref[...], w_ref[...], preferred_element_type=F32)
    if final_norm:
        y = y * lax.rsqrt(jnp.mean(y * y, axis=-1, keepdims=True) + EPS) * g_ref[...]
    o_ref[...] = y


def _out_proj(x2, m, w_bf, final_g, final_norm, tm=512):
    t = x2.shape[0]
    blk = pl.BlockSpec((tm, D_MODEL), lambda i: (i, 0))
    return pl.pallas_call(
        functools.partial(_out_proj_kernel, final_norm=final_norm),
        grid=(t // tm,),
        in_specs=[blk, blk,
                  pl.BlockSpec((D_MODEL, D_MODEL), lambda i: (0, 0)),
                  pl.BlockSpec((1, D_MODEL), lambda i: (0, 0))],
        out_specs=blk,
        out_shape=jax.ShapeDtypeStruct((t, D_MODEL), F32),
        compiler_params=_cparams(("parallel",)),
        name="out_proj",
    )(x2, m, w_bf, final_g)


def _lb_kernel(x_ref, o_ref):
    x = x_ref[...]
    e = jnp.exp(x - jnp.max(x, axis=0, keepdims=True))
    sm = e / jnp.sum(e, axis=0, keepdims=True)
    run = jnp.zeros_like(sm[0])
    for l in range(DEPTH):
        run = run + sm[l]
        o_ref[l] = run - sm[0]


def _lower_bounds(hg_lb):
    return pl.pallas_call(
        _lb_kernel,
        out_shape=jax.ShapeDtypeStruct(hg_lb.shape, F32),
        name="hg_lower_bounds",
    )(hg_lb)


def _layer(x, lw, final_g, final_norm):
    bsz, seq_len, _ = x.shape
    x2 = x.reshape(bsz * seq_len, D_MODEL)
    p2 = _in_proj(x2, lw["norm_g"], lw["w_in"])
    p3 = p2.reshape(bsz, seq_len, N_IN)
    ya = _conv_branch(p3, lw["conv_w"], lw["conv_b"], lw["conv_ln_g"], lw["conv_ln_b"])
    yb = _pool_branch(p3, lw["pool_w"], lw["pool_scale"])
    o_f, o_b = _hg_scan(p3, lw["lb"], lw["hg_nm"], lw["hg_mk"])
    yc = _hg_post(o_f, o_b, p3, lw["hg_norm_g"])
    y5 = _s5_scan(p3, *lw["s5"], lw["s5_d"], lw["s5_perm"])
    yd = _s5_post(y5, p3, lw["s5_glu_w"], lw["s5_glu_b"])
    ys = [y.reshape(bsz * seq_len, W_BRANCH) for y in (ya, yb, yc, yd)]
    m = _merge(ys, p2, lw["w_branch_out"])
    out = _out_proj(x2, m, lw["w_out"], final_g, final_norm)
    return out.reshape(bsz, seq_len, D_MODEL)


def kernel(x_prompt, x_sample, norm_g, w_in, conv_w, conv_b, conv_ln_g, conv_ln_b, w_a_out, pool_w, pool_scale, w_b_out, hg_lb, hg_norm_g, w_c_out, s5_a_re, s5_a_im, s5_log_dt, s5_b_re, s5_b_im, s5_c_re, s5_c_im, s5_d, s5_glu_w, s5_glu_b, w_d_out, w_out, final_g):
    lb_all = _lower_bounds(hg_lb)
    nm, mk = _hg_tables()
    nm_bf = jnp.asarray(nm, BF16)
    mk_bf = jnp.asarray(mk, BF16)
    perm_bf = jnp.asarray(_s5_perm(), BF16)
    layers = []
    for l in range(DEPTH):
        prep = _s5_prep(s5_a_re[l], s5_a_im[l], s5_log_dt[l], s5_b_re[l], s5_b_im[l])
        layers.append(dict(
            norm_g=norm_g[l][None], w_in=w_in[l].astype(BF16),
            conv_w=conv_w[l], conv_b=conv_b[l][None], conv_ln_g=conv_ln_g[l][None], conv_ln_b=conv_ln_b[l][None],
            pool_w=pool_w[l].astype(BF16), pool_scale=pool_scale[l][None],
            lb=lb_all[l], hg_nm=nm_bf, hg_mk=mk_bf, hg_norm_g=hg_norm_g[l][None],
            s5=_s5_layout(*prep, s5_c_re[l], s5_c_im[l]), s5_d=s5_d[l][None],
            s5_perm=perm_bf,
            s5_glu_w=s5_glu_w[l].astype(BF16), s5_glu_b=s5_glu_b[l][None],
            w_branch_out=jnp.stack([w_a_out[l], w_b_out[l], w_c_out[l], w_d_out[l]]).astype(BF16),
            w_out=w_out[l].astype(BF16)))
    fg = final_g[None]

    def run(x):
        for l in range(DEPTH):
            x = _layer(x, layers[l], fg, l == DEPTH - 1)
        return x

    return (run(x_prompt), run(x_sample))
```

```python
import functools
import math

import numpy as np
import jax
import jax.numpy as jnp
from jax import lax
from jax.experimental import pallas as pl
from jax.experimental.pallas import tpu as pltpu

F32 = jnp.float32
BF16 = jnp.bfloat16

D_MODEL = 2048
DEPTH = 2
W_BRANCH = 1024
N_BRANCH = 4
N_IN = 12 * W_BRANCH + N_BRANCH * D_MODEL
CONV_WIDTH = 31
CONV_PAD = CONV_WIDTH // 2
POOL_WINDOWS = (2, 4, 8, 16)
POOL_GROUP = W_BRANCH // len(POOL_WINDOWS)
HG_HEAD_DIM = 128
HG_HEADS = W_BRANCH // HG_HEAD_DIM
S5_GROUP_CH = 16
S5_GROUPS = W_BRANCH // S5_GROUP_CH
S5_STATE = 64
EPS = 1e-6

LANES = 128
SUBLANES = 8
VMEM_LIMIT = 52 * 1024 * 1024

COL_A_VAL, COL_A_GLU, COL_A_GATE, COL_B_IN, COL_B_GATE = 0, 1, 2, 3, 4
COL_C_Q, COL_C_FF, COL_C_FB, COL_C_I, COL_C_GATE, COL_D_IN, COL_D_GATE = 5, 6, 7, 8, 9, 10, 11
COL_R = 12

HALO = 16
CONV_LANES = 256
HG_TILE = 256
HG_HALF = HG_TILE // 2
HG_HPS = 4
HG_LEVELS = 8
HG_NROWS = 4 * HG_TILE
S5_TILE = 512
S5_SUB = S5_TILE // SUBLANES
S5_LT = W_BRANCH // LANES
S5_GPT = LANES // S5_GROUP_CH
S5_SW = S5_GPT * S5_STATE


def _sigmoid(x):
    return 1.0 / (1.0 + jnp.exp(-x))


def _silu(x):
    return x * _sigmoid(x)


def _cparams(sem):
    return pltpu.CompilerParams(dimension_semantics=sem, vmem_limit_bytes=VMEM_LIMIT)


def _in_proj_kernel(x_ref, g_ref, w_ref, o_ref, h_ref):
    @pl.when(pl.program_id(1) == 0)
    def _():
        x = x_ref[...]
        ms = jnp.mean(x * x, axis=-1, keepdims=True)
        h_ref[...] = (x * lax.rsqrt(ms + EPS) * g_ref[...]).astype(BF16)

    o_ref[...] = jnp.dot(h_ref[...], w_ref[...], preferred_element_type=F32)


def _in_proj(x2, g, w_bf, tm=1024, tn=1024):
    t = x2.shape[0]
    return pl.pallas_call(
        _in_proj_kernel,
        grid=(t // tm, N_IN // tn),
        in_specs=[pl.BlockSpec((tm, D_MODEL), lambda i, j: (i, 0)),
                  pl.BlockSpec((1, D_MODEL), lambda i, j: (0, 0)),
                  pl.BlockSpec((D_MODEL, tn), lambda i, j: (0, j))],
        out_specs=pl.BlockSpec((tm, tn), lambda i, j: (i, j)),
        out_shape=jax.ShapeDtypeStruct((t, N_IN), F32),
        scratch_shapes=[pltpu.VMEM((tm, D_MODEL), BF16)],
        compiler_params=_cparams(("parallel", "arbitrary")),
        name="in_proj",
    )(x2, g, w_bf)


def _conv_kernel(main_ref, gate_ref, prev_ref, next_ref, cw_ref, cb_ref, lg_ref, lb_ref, o_ref,
                 us_scr, wb_scr, *, tt, rc):
    t = pl.program_id(1)
    n_t = pl.num_programs(1)

    def glu(ref):
        return ref[0, :, :W_BRANCH] * _sigmoid(ref[0, :, W_BRANCH:])

    us_scr[0, HALO:HALO + tt, :] = glu(main_ref)
    us_scr[0, 0:HALO, :] = jnp.where(t > 0, glu(prev_ref), 0.0)
    us_scr[0, HALO + tt:2 * HALO + tt, :] = jnp.where(t < n_t - 1, glu(next_ref), 0.0)
    n = tt + 2 * HALO - SUBLANES
    for s in range(1, SUBLANES):
        us_scr[s, 0:n, :] = us_scr[0, s:s + n, :]
    for j in range(CONV_WIDTH):
        wb_scr[j] = jnp.broadcast_to(cw_ref[j:j + 1, :], (SUBLANES, W_BRANCH))

    def body(c, carry):
        r0 = pl.multiple_of(c * rc, rc)
        parts = []
        for lc in range(W_BRANCH // CONV_LANES):
            lanes = slice(lc * CONV_LANES, (lc + 1) * CONV_LANES)
            acc = jnp.zeros((rc // SUBLANES, SUBLANES, CONV_LANES), F32)
            for j in range(CONV_WIDTH):
                off = HALO - CONV_PAD + j
                s = off % SUBLANES
                tap = us_scr[s, pl.ds(r0 + (off - s), rc), lanes]
                acc = acc + tap.reshape(rc // SUBLANES, SUBLANES, CONV_LANES) * wb_scr[j, :, lanes]
            parts.append(acc.reshape(rc, CONV_LANES))
        y = jnp.concatenate(parts, axis=-1) + cb_ref[...]
        mu = jnp.mean(y, axis=-1, keepdims=True)
        yc = y - mu
        var = jnp.mean(yc * yc, axis=-1, keepdims=True)
        yn = yc * lax.rsqrt(var + EPS) * lg_ref[...] + lb_ref[...]
        out = _silu(yn) * _silu(gate_ref[0, pl.ds(r0, rc), :])
        o_ref[0, pl.ds(r0, rc), :] = out.astype(o_ref.dtype)
        return carry

    lax.fori_loop(0, tt // rc, body, 0)


def _halo_specs(tt, width, col_blk, seq_len):
    per = tt // HALO
    last = seq_len // HALO - 1
    prev = pl.BlockSpec((1, HALO, width), lambda b, t: (b, jnp.maximum(t * per - 1, 0), col_blk))
    nxt = pl.BlockSpec((1, HALO, width), lambda b, t: (b, jnp.minimum((t + 1) * per, last), col_blk))
    return prev, nxt


def _conv_branch(p3, cw, cb, lg, lb, tt=512, rc=32):
    bsz, seq_len, _ = p3.shape
    prev, nxt = _halo_specs(tt, 2 * W_BRANCH, 0, seq_len)
    vec = pl.BlockSpec((1, W_BRANCH), lambda b, t: (0, 0))
    return pl.pallas_call(
        functools.partial(_conv_kernel, tt=tt, rc=rc),
        grid=(bsz, seq_len // tt),
        in_specs=[pl.BlockSpec((1, tt, 2 * W_BRANCH), lambda b, t: (b, t, 0)),
                  pl.BlockSpec((1, tt, W_BRANCH), lambda b, t: (b, t, COL_A_GATE)),
                  prev, nxt,
                  pl.BlockSpec((CONV_WIDTH, W_BRANCH), lambda b, t: (0, 0)),
                  vec, vec, vec],
        out_specs=pl.BlockSpec((1, tt, W_BRANCH), lambda b, t: (b, t, 0)),
        out_shape=jax.ShapeDtypeStruct((bsz, seq_len, W_BRANCH), BF16),
        scratch_shapes=[pltpu.VMEM((SUBLANES, tt + 2 * HALO, W_BRANCH), F32),
                        pltpu.VMEM((CONV_WIDTH, SUBLANES, W_BRANCH), F32)],
        compiler_params=_cparams(("parallel", "parallel")),
        name="conv_branch",
    )(p3, p3, p3, p3, cw, cb, lg, lb)


def _pool_kernel(main_ref, gate_ref, prev_ref, next_ref, pw_ref, ps_ref, o_ref, e_scr, s_scr,
                 *, tt, seq_len):
    t = pl.program_id(1)
    n_t = pl.num_programs(1)
    rows = tt + 2 * HALO
    e_scr[HALO:HALO + tt, :] = main_ref[0]
    e_scr[0:HALO, :] = jnp.where(t > 0, prev_ref[0], 0.0)
    e_scr[HALO + tt:rows, :] = jnp.where(t < n_t - 1, next_ref[0], 0.0)

    tpos = t * tt + lax.broadcasted_iota(jnp.int32, (tt, 1), 0)
    for g, win in enumerate(POOL_WINDOWS):
        lanes = slice(g * POOL_GROUP, (g + 1) * POOL_GROUP)
        n = rows - 2
        s_scr[1:1 + n, lanes] = e_scr[1:1 + n, lanes] + e_scr[0:n, lanes]
        half = 1
        while 2 * half < win:
            n = rows - 2 * half - 2 * half
            lo = 2 * half
            a = s_scr[lo + half:lo + half + n, lanes]
            b = s_scr[lo - half:lo - half + n, lanes]
            s_scr[lo:lo + n, lanes] = a + b
            half *= 2
        left = win // 2
        right = win - 1 - left
        cnt = (jnp.minimum(tpos + right, seq_len - 1) + 1 - jnp.maximum(tpos - left, 0)).astype(F32)
        u = e_scr[HALO:HALO + tt, lanes]
        d = s_scr[HALO:HALO + tt, lanes] / cnt - u
        y = jnp.dot(d.astype(BF16), pw_ref[g], preferred_element_type=F32)
        y = y * ps_ref[:, lanes] * _silu(gate_ref[0, :, lanes])
        o_ref[0, :, lanes] = y.astype(o_ref.dtype)


def _pool_branch(p3, pw_bf, ps, tt=512):
    bsz, seq_len, _ = p3.shape
    prev, nxt = _halo_specs(tt, W_BRANCH, COL_B_IN, seq_len)
    return pl.pallas_call(
        functools.partial(_pool_kernel, tt=tt, seq_len=seq_len),
        grid=(bsz, seq_len // tt),
        in_specs=[pl.BlockSpec((1, tt, W_BRANCH), lambda b, t: (b, t, COL_B_IN)),
                  pl.BlockSpec((1, tt, W_BRANCH), lambda b, t: (b, t, COL_B_GATE)),
                  prev, nxt,
                  pl.BlockSpec((len(POOL_WINDOWS), POOL_GROUP, POOL_GROUP), lambda b, t: (0, 0, 0)),
                  pl.BlockSpec((1, W_BRANCH), lambda b, t: (0, 0))],
        out_specs=pl.BlockSpec((1, tt, W_BRANCH), lambda b, t: (b, t, 0)),
        out_shape=jax.ShapeDtypeStruct((bsz, seq_len, W_BRANCH), BF16),
        scratch_shapes=[pltpu.VMEM((tt + 2 * HALO, W_BRANCH), F32),
                        pltpu.VMEM((tt + 2 * HALO, W_BRANCH), F32)],
        compiler_params=_cparams(("parallel", "parallel")),
        name="pool_branch",
    )(p3, p3, p3, p3, pw_bf, ps)


def _hg_tables():
    c = HG_TILE
    t = np.arange(c)
    rr = np.arange(c)[None, :]
    vstart = (t // SUBLANES) * SUBLANES
    pref = [t, vstart + 7, vstart + 3, vstart + np.where(t % SUBLANES < 4, 1, 5)]
    nmat = np.stack([rr <= p[:, None] for p in pref]).astype(np.float32)
    masks = np.zeros((HG_LEVELS, HG_HALF, HG_HALF), np.float32)
    masks[0] = np.eye(HG_HALF)
    th = np.arange(HG_HALF)
    for lvl in range(HG_LEVELS - 1):
        half = 1 << lvl
        bh = th // (2 * half)
        lh = (th % (2 * half)) >= half
        masks[1 + lvl] = (bh[:, None] == bh[None, :]) & lh[:, None] & (~lh)[None, :]
    nm = np.stack([nmat, nmat[:, ::-1, ::-1]]).reshape(2, HG_NROWS, c)
    mk = np.stack([masks, masks[:, ::-1, ::-1]])
    return nm, mk


def _dot_nt(a, b):
    return lax.dot_general(a, b, (((1,), (1,)), ((), ())), preferred_element_type=F32)


def _hg_ref_rows(lvl, d):
    out = []
    for m in range(HG_TILE // SUBLANES):
        block = 2 << lvl
        mid = (m * SUBLANES // block) * block + block // 2
        out.append(mid // SUBLANES - 1 if d == 0 else mid // SUBLANES)
    return out


def _hg_gates(q, z, v, lbv, nm_ref, d):
    c = HG_TILE
    sg = _sigmoid(z)
    f = lbv + (1.0 - lbv) * sg
    lf = jnp.log2(f)
    kk = (1.0 - lbv) * (1.0 - sg)
    hi = lf.astype(BF16)
    lo = (lf - hi.astype(F32)).astype(BF16)
    tab = jnp.dot(nm_ref[d], jnp.concatenate([hi, lo], axis=1), preferred_element_type=F32)
    tab = tab[:, :HG_HEAD_DIM] + tab[:, HG_HEAD_DIM:]
    b, b_grp, b_l2, b_l1 = (tab[i * c:(i + 1) * c] for i in range(4))
    b_end = b_grp[c - 1:c] if d == 0 else b_grp[0:1]
    n_g = c // SUBLANES
    sub = lax.broadcasted_iota(jnp.int32, (1, SUBLANES, HG_HEAD_DIM), 1)
    q3 = q.reshape(n_g, SUBLANES, HG_HEAD_DIM)
    k3 = kk.reshape(n_g, SUBLANES, HG_HEAD_DIM)
    xs = []
    for lvl in range(HG_LEVELS):
        if (1 << lvl) < SUBLANES:
            later = (jnp.right_shift(sub, lvl) & 1) == (1 - d)
            base = jnp.where(later, q3, k3).reshape(c, HG_HEAD_DIM)
        else:
            pick = [((m * SUBLANES) >> lvl) & 1 == (1 - d) for m in range(n_g)]
            base = jnp.concatenate([(q3 if p else k3)[m] for m, p in enumerate(pick)], axis=0)
        if lvl == 0:
            later0 = jnp.broadcast_to(later, (n_g, SUBLANES, HG_HEAD_DIM)).reshape(c, HG_HEAD_DIM)
            xs.append(jnp.where(later0, base * f, base))
            continue
        if lvl == 1:
            ref = b_l1
        elif lvl == 2:
            ref = b_l2
        else:
            ref = jnp.concatenate([b_grp[g * SUBLANES:(g + 1) * SUBLANES] for g in _hg_ref_rows(lvl, d)], axis=0)
        neg_abs = pltpu.bitcast(pltpu.bitcast(b - ref, jnp.uint32) | jnp.uint32(0x80000000), F32)
        xs.append(base * jnp.exp2(neg_abs))
    return dict(q=q, kk=kk, v=v, xs=xs, q_in=q * jnp.exp2(b), k_out=kk * jnp.exp2(b_end - b),
                bend=jnp.exp2(b_end))


def _hg_kernel(qf_ref, zf_ref, vf_ref, qb_ref, zb_ref, vb_ref, lb_ref, nm_ref, mk_ref,
               of_ref, ob_ref, sf_scr, sb_scr):
    @pl.when(pl.program_id(2) == 0)
    def _():
        sf_scr[...] = jnp.zeros_like(sf_scr)
        sb_scr[...] = jnp.zeros_like(sb_scr)

    for hh in range(HG_HPS):
        _hg_head(hh, (qf_ref, zf_ref, vf_ref), (qb_ref, zb_ref, vb_ref), lb_ref, nm_ref, mk_ref,
                 (of_ref, ob_ref), (sf_scr, sb_scr))


def _hg_head(hh, ins_f, ins_b, lb_ref, nm_ref, mk_ref, o_refs, st_refs):
    c = HG_TILE
    halves = (slice(0, HG_HALF), slice(HG_HALF, c))
    lanes = slice(hh * HG_HEAD_DIM, (hh + 1) * HG_HEAD_DIM)
    ins = (ins_f, ins_b)
    g = [_hg_gates(ins[d][0][0, :, lanes], ins[d][1][0, :, lanes], ins[d][2][0, :, lanes],
                   lb_ref[d:d + 1, lanes], nm_ref, d) for d in range(2)]
    scores = [[[], []] for _ in range(2)]
    for d in range(2):
        qb = g[d]["q"].astype(BF16)
        kb = g[d]["kk"].astype(BF16)
        for i, rs in enumerate(halves):
            scores[d][i].append((_dot_nt(qb[rs], kb[rs]), 0))
        for lvl in range(HG_LEVELS - 1):
            x = g[d]["xs"][lvl].astype(BF16)
            for i, rs in enumerate(halves):
                scores[d][i].append((_dot_nt(x[rs], x[rs]), 1 + lvl))
        x = g[d]["xs"][HG_LEVELS - 1].astype(BF16)
        late, early = (1, 0) if d == 0 else (0, 1)
        scores[d][late].append((_dot_nt(x[halves[late]], x[halves[early]]), None))
    outs = []
    for d in range(2):
        vb = g[d]["v"].astype(BF16)
        late, early = (1, 0) if d == 0 else (0, 1)
        acc = []
        for i, rs in enumerate(halves):
            sm = [s.astype(BF16) * mk_ref[d, m] if m is not None else s.astype(BF16) for s, m in scores[d][i]]
            vs = [vb[rs]] * HG_LEVELS + ([vb[halves[early]]] if i == late else [])
            acc.append(jnp.dot(jnp.concatenate(sm, axis=1), jnp.concatenate(vs, axis=0),
                               preferred_element_type=F32))
        outs.append(acc)
    for d in range(2):
        st = st_refs[d][hh]
        o_refs[d][0, :, lanes] = (jnp.concatenate(outs[d], axis=0)
                                  + _dot_nt(g[d]["q_in"].astype(BF16), st.astype(BF16)))
        vt = jnp.transpose(g[d]["v"]).astype(BF16)
        st_refs[d][hh] = st * g[d]["bend"] + jnp.dot(vt, g[d]["k_out"].astype(BF16),
                                                     preferred_element_type=F32)


def _hg_scan(p3, lb, nm_bf, mk_bf):
    bsz, seq_len, _ = p3.shape
    tt = HG_TILE
    n_t = seq_len // tt
    hw = HG_HPS * HG_HEAD_DIM
    cpb = W_BRANCH // hw

    def fspec(col):
        return pl.BlockSpec((1, tt, hw), lambda b, h, k: (b, k, col * cpb + h))

    def bspec(col):
        return pl.BlockSpec((1, tt, hw), lambda b, h, k: (b, n_t - 1 - k, col * cpb + h))

    out = jax.ShapeDtypeStruct((bsz, seq_len, W_BRANCH), F32)
    return pl.pallas_call(
        _hg_kernel,
        grid=(bsz, HG_HEADS // HG_HPS, n_t),
        in_specs=[fspec(COL_C_Q), fspec(COL_C_FF), fspec(COL_C_I),
                  bspec(COL_C_Q), bspec(COL_C_FB), bspec(COL_C_I),
                  pl.BlockSpec((2, hw), lambda b, h, k: (0, h)),
                  pl.BlockSpec((2, HG_NROWS, HG_TILE), lambda b, h, k: (0, 0, 0)),
                  pl.BlockSpec((2, HG_LEVELS, HG_HALF, HG_HALF), lambda b, h, k: (0, 0, 0, 0))],
        out_specs=[pl.BlockSpec((1, tt, hw), lambda b, h, k: (b, k, h)),
                   pl.BlockSpec((1, tt, hw), lambda b, h, k: (b, n_t - 1 - k, h))],
        out_shape=[out, out],
        scratch_shapes=[pltpu.VMEM((HG_HPS, HG_HEAD_DIM, HG_HEAD_DIM), F32),
                        pltpu.VMEM((HG_HPS, HG_HEAD_DIM, HG_HEAD_DIM), F32)],
        compiler_params=_cparams(("parallel", "parallel", "arbitrary")),
        name="hgrn2_scan",
    )(p3, p3, p3, p3, p3, p3, lb, nm_bf, mk_bf)


def _hg_post_kernel(of_ref, ob_ref, gate_ref, g_ref, o_ref):
    for h in range(HG_HEADS):
        lanes = slice(h * HG_HEAD_DIM, (h + 1) * HG_HEAD_DIM)
        o = of_ref[0, :, lanes] + ob_ref[0, :, lanes]
        o = o * lax.rsqrt(jnp.mean(o * o, axis=-1, keepdims=True) + EPS)
        y = o * g_ref[:, lanes] * _silu(gate_ref[0, :, lanes])
        o_ref[0, :, lanes] = y.astype(o_ref.dtype)


def _hg_post(o_f, o_b, p3, norm_g, tt=512):
    bsz, seq_len, _ = p3.shape
    blk = pl.BlockSpec((1, tt, W_BRANCH), lambda b, t: (b, t, 0))
    return pl.pallas_call(
        _hg_post_kernel,
        grid=(bsz, seq_len // tt),
        in_specs=[blk, blk,
                  pl.BlockSpec((1, tt, W_BRANCH), lambda b, t: (b, t, COL_C_GATE)),
                  pl.BlockSpec((1, W_BRANCH), lambda b, t: (0, 0))],
        out_specs=blk,
        out_shape=jax.ShapeDtypeStruct((bsz, seq_len, W_BRANCH), BF16),
        compiler_params=_cparams(("parallel", "parallel")),
        name="hgrn2_post",
    )(o_f, o_b, p3, norm_g)


def _s5_prep_kernel(are_ref, aim_ref, ldt_ref, bre_ref, bim_ref,
                    abr_ref, abi_ref, bbr_ref, bbi_ref, pwr_ref, pwi_ref):
    a_re = are_ref[...]
    a_im = aim_ref[...]
    dt = jnp.exp(ldt_ref[...])
    mag = jnp.exp(dt * a_re)
    ang = dt * a_im
    ab_re = mag * jnp.cos(ang)
    ab_im = mag * jnp.sin(ang)
    den = a_re * a_re + a_im * a_im
    x_ = ab_re - 1.0
    y_ = ab_im
    g_re = (x_ * a_re + y_ * a_im) / den
    g_im = (y_ * a_re - x_ * a_im) / den
    abr_ref[...] = ab_re
    abi_ref[...] = ab_im
    for c in range(S5_GROUP_CH):
        b_re = bre_ref[c]
        b_im = bim_ref[c]
        bbr_ref[c] = g_re * b_re - g_im * b_im
        bbi_ref[c] = g_re * b_im + g_im * b_re
    p_re = ab_re
    p_im = ab_im
    for i in range(S5_SUB):
        pwr_ref[i] = p_re
        pwi_ref[i] = p_im
        p_re, p_im = p_re * ab_re - p_im * ab_im, p_re * ab_im + p_im * ab_re


def _s5_prep(a_re, a_im, log_dt, b_re, b_im):
    n = 2 * S5_GROUPS
    sp = jax.ShapeDtypeStruct((n, S5_STATE), F32)
    sb = jax.ShapeDtypeStruct((S5_GROUP_CH, n, S5_STATE), F32)
    spw = jax.ShapeDtypeStruct((S5_SUB, n, S5_STATE), F32)
    return pl.pallas_call(
        _s5_prep_kernel,
        out_shape=[sp, sp, sb, sb, spw, spw],
        name="s5_prep",
    )(a_re.reshape(n, S5_STATE), a_im.reshape(n, S5_STATE), log_dt.reshape(n, 1),
      jnp.transpose(b_re.reshape(n, S5_STATE, S5_GROUP_CH), (2, 0, 1)),
      jnp.transpose(b_im.reshape(n, S5_STATE, S5_GROUP_CH), (2, 0, 1)))


def _s5_layout(abr, abi, bbr, bbi, pwr, pwi, c_re, c_im):
    eye = jnp.eye(S5_GPT, dtype=F32)

    def tiles(x):
        return x.reshape(2, S5_LT, 1, S5_SW)

    a_t = jnp.stack([tiles(abr), tiles(abi)], axis=1)

    def pw_tiles(x):
        return jnp.transpose(x.reshape(S5_SUB, 2, S5_LT, S5_SW), (1, 2, 0, 3))

    pw_t = jnp.stack([pw_tiles(pwr), pw_tiles(pwi)], axis=1)

    def b_bd(x):
        x = jnp.transpose(x, (1, 0, 2)).reshape(2, S5_LT, S5_GPT, S5_GROUP_CH, S5_STATE)
        bd = x[:, :, :, :, None, :] * eye[None, None, :, None, :, None]
        return bd.reshape(2, S5_LT, LANES, S5_SW)

    b_t = jnp.concatenate([b_bd(bbr), b_bd(bbi)], axis=-1).astype(BF16)

    def c_bd(x):
        x = jnp.transpose(x.reshape(2, S5_LT, S5_GPT, S5_GROUP_CH, S5_STATE), (0, 1, 2, 4, 3))
        bd = x[:, :, :, :, None, :] * eye[None, None, :, None, :, None]
        return bd.reshape(2, S5_LT, S5_SW, LANES).astype(BF16)

    return a_t, pw_t, b_t, c_bd(c_re), c_bd(-c_im)


def _s5_perm():
    n = SUBLANES * SUBLANES
    r = np.arange(n)
    p = np.zeros((n, n), np.float32)
    p[r, (r % SUBLANES) * SUBLANES + r // SUBLANES] = 1.0
    return p


def _s5_to_wide(x, swap):
    g = SUBLANES

    def grp(p, q):
        k = (q * g + p) if swap else (p * g + q)
        return x[k * g:(k + 1) * g, :]

    return jnp.concatenate([jnp.concatenate([grp(p, q) for q in range(g)], axis=1) for p in range(g)], axis=0)


def _s5_from_wide(x, swap):
    g = SUBLANES
    n = x.shape[1] // g
    order = [(k % g, k // g) if swap else (k // g, k % g) for k in range(g * g)]
    return jnp.concatenate([x[p * g:(p + 1) * g, q * n:(q + 1) * n] for p, q in order], axis=0)


def _s5_kernel(u_ref, a_ref, pw_ref, b_ref, cre_ref, cimn_ref, d_ref, pm_ref, o_ref,
               hre_scr, him_scr, hbr_scr, hbi_scr, hin_re_scr, hin_im_scr, pwb_scr, *, seq_len):
    n_t = seq_len // S5_TILE
    o_ref[0] = u_ref[0] * d_ref[...]
    for d in range(2):
        for c in range(2):
            for i in range(S5_SUB):
                pwb_scr[d, c, i] = jnp.broadcast_to(pw_ref[d, c, 0, i:i + 1, :], (SUBLANES, S5_SW))
    hbr_scr[...] = jnp.zeros_like(hbr_scr)
    hbi_scr[...] = jnp.zeros_like(hbi_scr)
    zero_row = jnp.zeros((1, S5_SW), F32)
    zero_blk = jnp.zeros((SUBLANES, S5_SW), F32)
    ar = [jnp.broadcast_to(a_ref[d, 0, 0], (SUBLANES, S5_SW)) for d in range(2)]
    ai = [jnp.broadcast_to(a_ref[d, 1, 0], (SUBLANES, S5_SW)) for d in range(2)]
    asr = [pw_ref[d, 0, 0, S5_SUB - 1:S5_SUB, :] for d in range(2)]
    asi = [pw_ref[d, 1, 0, S5_SUB - 1:S5_SUB, :] for d in range(2)]

    def tile_rows(k, d):
        kt = k if d == 0 else n_t - 1 - k
        return pl.ds(pl.multiple_of(kt * S5_TILE, S5_TILE), S5_TILE)

    def project(k, slot):
        us = [_s5_to_wide(u_ref[0, tile_rows(k, d), :].astype(BF16), False) for d in range(2)]
        ups = [jnp.dot(pm_ref[...], us[d], preferred_element_type=F32).astype(BF16) for d in range(2)]
        bus = [jnp.dot(_s5_from_wide(ups[d], True), b_ref[d, 0], preferred_element_type=F32) for d in range(2)]
        for d in range(2):
            hre_scr[slot, d] = bus[d][:, :S5_SW]
            him_scr[slot, d] = bus[d][:, S5_SW:]

    def readout(k, slot):
        ys = [jnp.dot(hbr_scr[slot, d], cre_ref[d, 0], preferred_element_type=F32)
              + jnp.dot(hbi_scr[slot, d], cimn_ref[d, 0], preferred_element_type=F32) for d in range(2)]
        ycs = []
        for d in range(2):
            y_hi = ys[d].astype(BF16)
            y_lo = (ys[d] - y_hi.astype(F32)).astype(BF16)
            ycs.append(jnp.concatenate([_s5_to_wide(y_hi, True), _s5_to_wide(y_lo, True)], axis=1))
        yns = [jnp.dot(pm_ref[...], ycs[d], preferred_element_type=F32) for d in range(2)]
        for d in range(2):
            yn = _s5_from_wide(yns[d][:, :SUBLANES * LANES] + yns[d][:, SUBLANES * LANES:], False)
            rows = tile_rows(k, d)
            o_ref[0, rows, :] = o_ref[0, rows, :] + yn

    def blk(i):
        return slice(i * SUBLANES, (i + 1) * SUBLANES)

    def tile_step(k, slot, carry):
        other = 1 - slot
        project(jnp.minimum(k + 1, n_t - 1), other)
        h = [zero_blk] * 4
        for i in range(S5_SUB):
            for d in range(2):
                rows = blk(i if d == 0 else S5_SUB - 1 - i)
                hr, hi = h[2 * d], h[2 * d + 1]
                nr = ar[d] * hr - ai[d] * hi + hre_scr[slot, d, rows, :]
                ni = ar[d] * hi + ai[d] * hr + him_scr[slot, d, rows, :]
                hre_scr[slot, d, rows, :] = nr
                him_scr[slot, d, rows, :] = ni
                h[2 * d], h[2 * d + 1] = nr, ni
        new_carry = []
        for d in range(2):
            c_re, c_im = carry[2 * d], carry[2 * d + 1]
            for j in (range(SUBLANES) if d == 0 else range(SUBLANES - 1, -1, -1)):
                hin_re_scr[d, j:j + 1, :] = c_re
                hin_im_scr[d, j:j + 1, :] = c_im
                er = h[2 * d][j:j + 1, :]
                ei = h[2 * d + 1][j:j + 1, :]
                c_re, c_im = er + asr[d] * c_re - asi[d] * c_im, ei + asr[d] * c_im + asi[d] * c_re
            new_carry += [c_re, c_im]
        hin_re = [hin_re_scr[d] for d in range(2)]
        hin_im = [hin_im_scr[d] for d in range(2)]
        for i2 in range(S5_SUB // 2):
            for d in range(2):
                res_re, res_im = [], []
                for i in (2 * i2, 2 * i2 + 1):
                    pi = i if d == 0 else S5_SUB - 1 - i
                    pr = pwb_scr[d, 0, pi]
                    pim = pwb_scr[d, 1, pi]
                    res_re.append(hre_scr[slot, d, blk(i), :] + pr * hin_re[d] - pim * hin_im[d])
                    res_im.append(him_scr[slot, d, blk(i), :] + pr * hin_im[d] + pim * hin_re[d])
                rows2 = slice(i2 * 2 * SUBLANES, (i2 + 1) * 2 * SUBLANES)
                hbr_scr[slot, d, rows2, :] = jnp.concatenate(res_re, axis=0).astype(BF16)
                hbi_scr[slot, d, rows2, :] = jnp.concatenate(res_im, axis=0).astype(BF16)
        readout(jnp.maximum(k - 1, 0), other)
        return tuple(new_carry)

    def tile_body(k2, carry):
        carry = tile_step(2 * k2, 0, carry)
        return tile_step(2 * k2 + 1, 1, carry)

    project(0, 0)
    lax.fori_loop(0, n_t // 2, tile_body, (zero_row,) * 4)
    readout(n_t - 1, (n_t - 1) % 2)


def _s5_scan(p3, a_t, pw_t, b_t, cre_t, cimn_t, d_skip, perm):
    bsz, seq_len, _ = p3.shape
    cpb = W_BRANCH // LANES
    return pl.pallas_call(
        functools.partial(_s5_kernel, seq_len=seq_len),
        grid=(S5_LT, bsz),
        in_specs=[pl.BlockSpec((1, seq_len, LANES), lambda l, b: (b, 0, COL_D_IN * cpb + l)),
                  pl.BlockSpec((2, 2, 1, 1, S5_SW), lambda l, b: (0, 0, l, 0, 0)),
                  pl.BlockSpec((2, 2, 1, S5_SUB, S5_SW), lambda l, b: (0, 0, l, 0, 0)),
                  pl.BlockSpec((2, 1, LANES, 2 * S5_SW), lambda l, b: (0, l, 0, 0)),
                  pl.BlockSpec((2, 1, S5_SW, LANES), lambda l, b: (0, l, 0, 0)),
                  pl.BlockSpec((2, 1, S5_SW, LANES), lambda l, b: (0, l, 0, 0)),
                  pl.BlockSpec((1, LANES), lambda l, b: (0, l)),
                  pl.BlockSpec((SUBLANES * SUBLANES, SUBLANES * SUBLANES), lambda l, b: (0, 0))],
        out_specs=pl.BlockSpec((1, seq_len, LANES), lambda l, b: (b, 0, l)),
        out_shape=jax.ShapeDtypeStruct((bsz, seq_len, W_BRANCH), F32),
        scratch_shapes=[pltpu.VMEM((2, 2, S5_TILE, S5_SW), F32),
                        pltpu.VMEM((2, 2, S5_TILE, S5_SW), F32),
                        pltpu.VMEM((2, 2, S5_TILE, S5_SW), BF16),
                        pltpu.VMEM((2, 2, S5_TILE, S5_SW), BF16),
                        pltpu.VMEM((2, SUBLANES, S5_SW), F32),
                        pltpu.VMEM((2, SUBLANES, S5_SW), F32),
                        pltpu.VMEM((2, 2, S5_SUB, SUBLANES, S5_SW), F32)],
        compiler_params=_cparams(("parallel", "parallel")),
        name="s5_scan",
    )(p3, a_t, pw_t, b_t, cre_t, cimn_t, d_skip, perm)


def _s5_post_kernel(y_ref, gate_ref, w_ref, b_ref, o_ref):
    y = y_ref[0]
    c0 = math.sqrt(2.0 / math.pi)
    z = 0.5 * y * (1.0 + jnp.tanh(c0 * (y + 0.044715 * (y * y * y))))
    lin = jnp.dot(z.astype(BF16), w_ref[...], preferred_element_type=F32) + b_ref[...]
    out = z * _sigmoid(lin) * _silu(gate_ref[0])
    o_ref[0] = out.astype(o_ref.dtype)


def _s5_post(y, p3, glu_w_bf, glu_b, tt=512):
    bsz, seq_len, _ = p3.shape
    blk = pl.BlockSpec((1, tt, W_BRANCH), lambda b, t: (b, t, 0))
    return pl.pallas_call(
        _s5_post_kernel,
        grid=(bsz, seq_len // tt),
        in_specs=[blk,
                  pl.BlockSpec((1, tt, W_BRANCH), lambda b, t: (b, t, COL_D_GATE)),
                  pl.BlockSpec((W_BRANCH, W_BRANCH), lambda b, t: (0, 0)),
                  pl.BlockSpec((1, W_BRANCH), lambda b, t: (0, 0))],
        out_specs=blk,
        out_shape=jax.ShapeDtypeStruct((bsz, seq_len, W_BRANCH), BF16),
        compiler_params=_cparams(("parallel", "parallel")),
        name="s5_post",
    )(y, p3, glu_w_bf, glu_b)


def _merge_kernel(ya_ref, yb_ref, yc_ref, yd_ref, r0_ref, r1_ref, r2_ref, r3_ref, w_ref, o_ref):
    acc = None
    for i, (y_ref, r_ref) in enumerate(((ya_ref, r0_ref), (yb_ref, r1_ref), (yc_ref, r2_ref), (yd_ref, r3_ref))):
        term = _sigmoid(r_ref[...]) * jnp.dot(y_ref[...], w_ref[i], preferred_element_type=F32)
        acc = term if acc is None else acc + term
    o_ref[...] = acc.astype(o_ref.dtype)


def _merge(ys, p2, w_stack_bf, tm=512, tn=1024):
    t = p2.shape[0]
    npb = D_MODEL // tn
    roff = COL_R * W_BRANCH // tn
    yspec = pl.BlockSpec((tm, W_BRANCH), lambda j, i: (i, 0))

    def rspec(br):
        return pl.BlockSpec((tm, tn), lambda j, i: (i, roff + br * npb + j))

    return pl.pallas_call(
        _merge_kernel,
        grid=(npb, t // tm),
        in_specs=[yspec, yspec, yspec, yspec, rspec(0), rspec(1), rspec(2), rspec(3),
                  pl.BlockSpec((N_BRANCH, W_BRANCH, tn), lambda j, i: (0, 0, j))],
        out_specs=pl.BlockSpec((tm, tn), lambda j, i: (i, j)),
        out_shape=jax.ShapeDtypeStruct((t, D_MODEL), BF16),
        compiler_params=_cparams(("parallel", "parallel")),
        name="merge",
    )(*ys, p2, p2, p2, p2, w_stack_bf)


def _out_proj_kernel(x_ref, m_ref, w_ref, g_ref, o_ref, *, final_norm):
    y = x_ref[...] + jnp.dot(m_ref[...], w_ref[...], preferred_element_type=F32)
    if final_norm:
        y = y * lax.rsqrt(jnp.mean(y * y, axis=-1, keepdims=True) + EPS) * g_ref[...]
    o_ref[...] = y


def _out_proj(x2, m, w_bf, final_g, final_norm, tm=512):
    t = x2.shape[0]
    blk = pl.BlockSpec((tm, D_MODEL), lambda i: (i, 0))
    return pl.pallas_call(
        functools.partial(_out_proj_kernel, final_norm=final_norm),
        grid=(t // tm,),
        in_specs=[blk, blk,
                  pl.BlockSpec((D_MODEL, D_MODEL), lambda i: (0, 0)),
                  pl.BlockSpec((1, D_MODEL), lambda i: (0, 0))],
        out_specs=blk,
        out_shape=jax.ShapeDtypeStruct((t, D_MODEL), F32),
        compiler_params=_cparams(("parallel",)),
        name="out_proj",
    )(x2, m, w_bf, final_g)


def _lb_kernel(x_ref, o_ref):
    x = x_ref[...]
    e = jnp.exp(x - jnp.max(x, axis=0, keepdims=True))
    sm = e / jnp.sum(e, axis=0, keepdims=True)
    run = jnp.zeros_like(sm[0])
    for l in range(DEPTH):
        run = run + sm[l]
        o_ref[l] = run - sm[0]


def _lower_bounds(hg_lb):
    return pl.pallas_call(
        _lb_kernel,
        out_shape=jax.ShapeDtypeStruct(hg_lb.shape, F32),
        name="hg_lower_bounds",
    )(hg_lb)


def _layer(x, lw, final_g, final_norm):
    bsz, seq_len, _ = x.shape
    x2 = x.reshape(bsz * seq_len, D_MODEL)
    p2 = _in_proj(x2, lw["norm_g"], lw["w_in"])
    p3 = p2.reshape(bsz, seq_len, N_IN)
    ya = _conv_branch(p3, lw["conv_w"], lw["conv_b"], lw["conv_ln_g"], lw["conv_ln_b"])
    yb = _pool_branch(p3, lw["pool_w"], lw["pool_scale"])
    o_f, o_b = _hg_scan(p3, lw["lb"], lw["hg_nm"], lw["hg_mk"])
    yc = _hg_post(o_f, o_b, p3, lw["hg_norm_g"])
    y5 = _s5_scan(p3, *lw["s5"], lw["s5_d"], lw["s5_perm"])
    yd = _s5_post(y5, p3, lw["s5_glu_w"], lw["s5_glu_b"])
    ys = [y.reshape(bsz * seq_len, W_BRANCH) for y in (ya, yb, yc, yd)]
    m = _merge(ys, p2, lw["w_branch_out"])
    out = _out_proj(x2, m, lw["w_out"], final_g, final_norm)
    return out.reshape(bsz, seq_len, D_MODEL)


def kernel(x_prompt, x_sample, norm_g, w_in, conv_w, conv_b, conv_ln_g, conv_ln_b, w_a_out, pool_w, pool_scale, w_b_out, hg_lb, hg_norm_g, w_c_out, s5_a_re, s5_a_im, s5_log_dt, s5_b_re, s5_b_im, s5_c_re, s5_c_im, s5_d, s5_glu_w, s5_glu_b, w_d_out, w_out, final_g):
    lb_all = _lower_bounds(hg_lb)
    nm, mk = _hg_tables()
    nm_bf = jnp.asarray(nm, BF16)
    mk_bf = jnp.asarray(mk, BF16)
    perm_bf = jnp.asarray(_s5_perm(), BF16)
    layers = []
    for l in range(DEPTH):
        prep = _s5_prep(s5_a_re[l], s5_a_im[l], s5_log_dt[l], s5_b_re[l], s5_b_im[l])
        layers.append(dict(
            norm_g=norm_g[l][None], w_in=w_in[l].astype(BF16),
            conv_w=conv_w[l], conv_b=conv_b[l][None], conv_ln_g=conv_ln_g[l][None], conv_ln_b=conv_ln_b[l][None],
            pool_w=pool_w[l].astype(BF16), pool_scale=pool_scale[l][None],
            lb=lb_all[l], hg_nm=nm_bf, hg_mk=mk_bf, hg_norm_g=hg_norm_g[l][None],
            s5=_s5_layout(*prep, s5_c_re[l], s5_c_im[l]), s5_d=s5_d[l][None],
            s5_perm=perm_bf,
            s5_glu_w=s5_glu_w[l].astype(BF16), s5_glu_b=s5_glu_b[l][None],
            w_branch_out=jnp.stack([w_a_out[l], w_b_out[l], w_c_out[l], w_d_out[l]]).astype(BF16),
            w_out=w_out[l].astype(BF16)))
    fg = final_g[None]

    def run(x):
        for l in range(DEPTH):
            x = _layer(x, layers[l], fg, l == DEPTH - 1)
        return x

    return (run(x_prompt), run(x_sample))
```

```python
import functools
import math

import numpy as np
import jax
import jax.numpy as jnp
from jax import lax
from jax.experimental import pallas as pl
from jax.experimental.pallas import tpu as pltpu

F32 = jnp.float32
BF16 = jnp.bfloat16

D_MODEL = 2048
DEPTH = 2
W_BRANCH = 1024
N_BRANCH = 4
N_IN = 12 * W_BRANCH + N_BRANCH * D_MODEL
CONV_WIDTH = 31
CONV_PAD = CONV_WIDTH // 2
POOL_WINDOWS = (2, 4, 8, 16)
POOL_GROUP = W_BRANCH // len(POOL_WINDOWS)
HG_HEAD_DIM = 128
HG_HEADS = W_BRANCH // HG_HEAD_DIM
S5_GROUP_CH = 16
S5_GROUPS = W_BRANCH // S5_GROUP_CH
S5_STATE = 64
EPS = 1e-6

LANES = 128
SUBLANES = 8
VMEM_LIMIT = 52 * 1024 * 1024

COL_A_VAL, COL_A_GLU, COL_A_GATE, COL_B_IN, COL_B_GATE = 0, 1, 2, 3, 4
COL_C_Q, COL_C_FF, COL_C_FB, COL_C_I, COL_C_GATE, COL_D_IN, COL_D_GATE = 5, 6, 7, 8, 9, 10, 11
COL_R = 12

HALO = 16
CONV_LANES = 256
HG_TILE = 256
HG_HALF = HG_TILE // 2
HG_HPS = 8
HG_LEVELS = 8
HG_NROWS = 4 * HG_TILE
S5_TILE = 512
S5_SUB = S5_TILE // SUBLANES
S5_BLK = 8
S5_GROUP = 4
S5_LT = W_BRANCH // LANES
S5_GPT = LANES // S5_GROUP_CH
S5_SW = S5_GPT * S5_STATE


def _sigmoid(x):
    return 1.0 / (1.0 + jnp.exp(-x))


def _silu(x):
    return x * _sigmoid(x)


def _cparams(sem):
    return pltpu.CompilerParams(dimension_semantics=sem, vmem_limit_bytes=VMEM_LIMIT)


def _in_proj_kernel(x_ref, g_ref, w_ref, o_ref, h_ref):
    @pl.when(pl.program_id(1) == 0)
    def _():
        x = x_ref[...]
        ms = jnp.mean(x * x, axis=-1, keepdims=True)
        h_ref[...] = (x * lax.rsqrt(ms + EPS) * g_ref[...]).astype(BF16)

    o_ref[...] = jnp.dot(h_ref[...], w_ref[...], preferred_element_type=F32)


def _in_proj(x2, g, w_bf, tm=1024, tn=1024):
    t = x2.shape[0]
    return pl.pallas_call(
        _in_proj_kernel,
        grid=(t // tm, N_IN // tn),
        in_specs=[pl.BlockSpec((tm, D_MODEL), lambda i, j: (i, 0)),
                  pl.BlockSpec((1, D_MODEL), lambda i, j: (0, 0)),
                  pl.BlockSpec((D_MODEL, tn), lambda i, j: (0, j))],
        out_specs=pl.BlockSpec((tm, tn), lambda i, j: (i, j)),
        out_shape=jax.ShapeDtypeStruct((t, N_IN), F32),
        scratch_shapes=[pltpu.VMEM((tm, D_MODEL), BF16)],
        compiler_params=_cparams(("parallel", "arbitrary")),
        name="in_proj",
    )(x2, g, w_bf)


def _conv_kernel(main_ref, gate_ref, prev_ref, next_ref, cw_ref, cb_ref, lg_ref, lb_ref, o_ref,
                 us_scr, wb_scr, *, tt, rc):
    t = pl.program_id(1)
    n_t = pl.num_programs(1)

    def glu(ref):
        return ref[0, :, :W_BRANCH] * _sigmoid(ref[0, :, W_BRANCH:])

    us_scr[0, HALO:HALO + tt, :] = glu(main_ref)
    us_scr[0, 0:HALO, :] = jnp.where(t > 0, glu(prev_ref), 0.0)
    us_scr[0, HALO + tt:2 * HALO + tt, :] = jnp.where(t < n_t - 1, glu(next_ref), 0.0)
    n = tt + 2 * HALO - SUBLANES
    for s in range(1, SUBLANES):
        us_scr[s, 0:n, :] = us_scr[0, s:s + n, :]
    for j in range(CONV_WIDTH):
        wb_scr[j] = jnp.broadcast_to(cw_ref[j:j + 1, :], (SUBLANES, W_BRANCH))

    def body(c, carry):
        r0 = pl.multiple_of(c * rc, rc)
        parts = []
        for lc in range(W_BRANCH // CONV_LANES):
            lanes = slice(lc * CONV_LANES, (lc + 1) * CONV_LANES)
            acc = jnp.zeros((rc // SUBLANES, SUBLANES, CONV_LANES), F32)
            for j in range(CONV_WIDTH):
                off = HALO - CONV_PAD + j
                s = off % SUBLANES
                tap = us_scr[s, pl.ds(r0 + (off - s), rc), lanes]
                acc = acc + tap.reshape(rc // SUBLANES, SUBLANES, CONV_LANES) * wb_scr[j, :, lanes]
            parts.append(acc.reshape(rc, CONV_LANES))
        y = jnp.concatenate(parts, axis=-1) + cb_ref[...]
        mu = jnp.mean(y, axis=-1, keepdims=True)
        yc = y - mu
        var = jnp.mean(yc * yc, axis=-1, keepdims=True)
        yn = yc * lax.rsqrt(var + EPS) * lg_ref[...] + lb_ref[...]
        out = _silu(yn) * _silu(gate_ref[0, pl.ds(r0, rc), :])
        o_ref[0, pl.ds(r0, rc), :] = out.astype(o_ref.dtype)
        return carry

    lax.fori_loop(0, tt // rc, body, 0)


def _halo_specs(tt, width, col_blk, seq_len):
    per = tt // HALO
    last = seq_len // HALO - 1
    prev = pl.BlockSpec((1, HALO, width), lambda b, t: (b, jnp.maximum(t * per - 1, 0), col_blk))
    nxt = pl.BlockSpec((1, HALO, width), lambda b, t: (b, jnp.minimum((t + 1) * per, last), col_blk))
    return prev, nxt


def _conv_branch(p3, cw, cb, lg, lb, tt=512, rc=32):
    bsz, seq_len, _ = p3.shape
    prev, nxt = _halo_specs(tt, 2 * W_BRANCH, 0, seq_len)
    vec = pl.BlockSpec((1, W_BRANCH), lambda b, t: (0, 0))
    return pl.pallas_call(
        functools.partial(_conv_kernel, tt=tt, rc=rc),
        grid=(bsz, seq_len // tt),
        in_specs=[pl.BlockSpec((1, tt, 2 * W_BRANCH), lambda b, t: (b, t, 0)),
                  pl.BlockSpec((1, tt, W_BRANCH), lambda b, t: (b, t, COL_A_GATE)),
                  prev, nxt,
                  pl.BlockSpec((CONV_WIDTH, W_BRANCH), lambda b, t: (0, 0)),
                  vec, vec, vec],
        out_specs=pl.BlockSpec((1, tt, W_BRANCH), lambda b, t: (b, t, 0)),
        out_shape=jax.ShapeDtypeStruct((bsz, seq_len, W_BRANCH), BF16),
        scratch_shapes=[pltpu.VMEM((SUBLANES, tt + 2 * HALO, W_BRANCH), F32),
                        pltpu.VMEM((CONV_WIDTH, SUBLANES, W_BRANCH), F32)],
        compiler_params=_cparams(("parallel", "parallel")),
        name="conv_branch",
    )(p3, p3, p3, p3, cw, cb, lg, lb)


def _pool_kernel(main_ref, gate_ref, prev_ref, next_ref, pw_ref, ps_ref, o_ref, e_scr, s_scr,
                 *, tt, seq_len):
    t = pl.program_id(1)
    n_t = pl.num_programs(1)
    rows = tt + 2 * HALO
    e_scr[HALO:HALO + tt, :] = main_ref[0]
    e_scr[0:HALO, :] = jnp.where(t > 0, prev_ref[0], 0.0)
    e_scr[HALO + tt:rows, :] = jnp.where(t < n_t - 1, next_ref[0], 0.0)

    tpos = t * tt + lax.broadcasted_iota(jnp.int32, (tt, 1), 0)
    for g, win in enumerate(POOL_WINDOWS):
        lanes = slice(g * POOL_GROUP, (g + 1) * POOL_GROUP)
        n = rows - 2
        s_scr[1:1 + n, lanes] = e_scr[1:1 + n, lanes] + e_scr[0:n, lanes]
        half = 1
        while 2 * half < win:
            n = rows - 2 * half - 2 * half
            lo = 2 * half
            a = s_scr[lo + half:lo + half + n, lanes]
            b = s_scr[lo - half:lo - half + n, lanes]
            s_scr[lo:lo + n, lanes] = a + b
            half *= 2
        left = win // 2
        right = win - 1 - left
        cnt = (jnp.minimum(tpos + right, seq_len - 1) + 1 - jnp.maximum(tpos - left, 0)).astype(F32)
        u = e_scr[HALO:HALO + tt, lanes]
        d = s_scr[HALO:HALO + tt, lanes] / cnt - u
        y = jnp.dot(d.astype(BF16), pw_ref[g], preferred_element_type=F32)
        y = y * ps_ref[:, lanes] * _silu(gate_ref[0, :, lanes])
        o_ref[0, :, lanes] = y.astype(o_ref.dtype)


def _pool_branch(p3, pw_bf, ps, tt=512):
    bsz, seq_len, _ = p3.shape
    prev, nxt = _halo_specs(tt, W_BRANCH, COL_B_IN, seq_len)
    return pl.pallas_call(
        functools.partial(_pool_kernel, tt=tt, seq_len=seq_len),
        grid=(bsz, seq_len // tt),
        in_specs=[pl.BlockSpec((1, tt, W_BRANCH), lambda b, t: (b, t, COL_B_IN)),
                  pl.BlockSpec((1, tt, W_BRANCH), lambda b, t: (b, t, COL_B_GATE)),
                  prev, nxt,
                  pl.BlockSpec((len(POOL_WINDOWS), POOL_GROUP, POOL_GROUP), lambda b, t: (0, 0, 0)),
                  pl.BlockSpec((1, W_BRANCH), lambda b, t: (0, 0))],
        out_specs=pl.BlockSpec((1, tt, W_BRANCH), lambda b, t: (b, t, 0)),
        out_shape=jax.ShapeDtypeStruct((bsz, seq_len, W_BRANCH), BF16),
        scratch_shapes=[pltpu.VMEM((tt + 2 * HALO, W_BRANCH), F32),
                        pltpu.VMEM((tt + 2 * HALO, W_BRANCH), F32)],
        compiler_params=_cparams(("parallel", "parallel")),
        name="pool_branch",
    )(p3, p3, p3, p3, pw_bf, ps)


def _hg_tables():
    c = HG_TILE
    t = np.arange(c)
    rr = np.arange(c)[None, :]
    vstart = (t // SUBLANES) * SUBLANES
    pref = [t, vstart + 7, vstart + 3, vstart + np.where(t % SUBLANES < 4, 1, 5)]
    nmat = np.stack([rr <= p[:, None] for p in pref]).astype(np.float32)
    masks = np.zeros((HG_LEVELS, HG_HALF, HG_HALF), np.float32)
    masks[0] = np.eye(HG_HALF)
    th = np.arange(HG_HALF)
    for lvl in range(HG_LEVELS - 1):
        half = 1 << lvl
        bh = th // (2 * half)
        lh = (th % (2 * half)) >= half
        masks[1 + lvl] = (bh[:, None] == bh[None, :]) & lh[:, None] & (~lh)[None, :]
    nm = np.stack([nmat, nmat[:, ::-1, ::-1]]).reshape(2, HG_NROWS, c)
    mk = np.stack([masks, masks[:, ::-1, ::-1]])
    return nm, mk


def _dot_nt(a, b):
    return lax.dot_general(a, b, (((1,), (1,)), ((), ())), preferred_element_type=F32)


def _hg_ref_rows(lvl, d):
    out = []
    for m in range(HG_TILE // SUBLANES):
        block = 2 << lvl
        mid = (m * SUBLANES // block) * block + block // 2
        out.append(mid // SUBLANES - 1 if d == 0 else mid // SUBLANES)
    return out


def _hg_gates(q, z, v, lbv, nm_ref, d):
    c = HG_TILE
    sg = _sigmoid(z)
    f = lbv + (1.0 - lbv) * sg
    lf = jnp.log2(f)
    kk = (1.0 - lbv) * (1.0 - sg)
    hi = lf.astype(BF16)
    lo = (lf - hi.astype(F32)).astype(BF16)
    tab = jnp.dot(nm_ref[d], jnp.concatenate([hi, lo], axis=1), preferred_element_type=F32)
    tab = tab[:, :HG_HEAD_DIM] + tab[:, HG_HEAD_DIM:]
    b, b_grp, b_l2, b_l1 = (tab[i * c:(i + 1) * c] for i in range(4))
    b_end = b_grp[c - 1:c] if d == 0 else b_grp[0:1]
    n_g = c // SUBLANES
    sub = lax.broadcasted_iota(jnp.int32, (1, SUBLANES, HG_HEAD_DIM), 1)
    q3 = q.reshape(n_g, SUBLANES, HG_HEAD_DIM)
    k3 = kk.reshape(n_g, SUBLANES, HG_HEAD_DIM)
    xs = []
    for lvl in range(HG_LEVELS):
        if (1 << lvl) < SUBLANES:
            later = (jnp.right_shift(sub, lvl) & 1) == (1 - d)
            base = jnp.where(later, q3, k3).reshape(c, HG_HEAD_DIM)
        else:
            pick = [((m * SUBLANES) >> lvl) & 1 == (1 - d) for m in range(n_g)]
            base = jnp.concatenate([(q3 if p else k3)[m] for m, p in enumerate(pick)], axis=0)
        if lvl == 0:
            later0 = jnp.broadcast_to(later, (n_g, SUBLANES, HG_HEAD_DIM)).reshape(c, HG_HEAD_DIM)
            xs.append(jnp.where(later0, base * f, base))
            continue
        if lvl == 1:
            ref = b_l1
        elif lvl == 2:
            ref = b_l2
        else:
            ref = jnp.concatenate([b_grp[g * SUBLANES:(g + 1) * SUBLANES] for g in _hg_ref_rows(lvl, d)], axis=0)
        neg_abs = pltpu.bitcast(pltpu.bitcast(b - ref, jnp.uint32) | jnp.uint32(0x80000000), F32)
        xs.append(base * jnp.exp2(neg_abs))
    return dict(q=q, kk=kk, v=v, xs=xs, q_in=q * jnp.exp2(b), k_out=kk * jnp.exp2(b_end - b),
                bend=jnp.exp2(b_end))


def _hg_kernel(qf_ref, zf_ref, vf_ref, qb_ref, zb_ref, vb_ref, lb_ref, nm_ref, mk_ref,
               of_ref, ob_ref, sf_scr, sb_scr):
    @pl.when(pl.program_id(2) == 0)
    def _():
        sf_scr[...] = jnp.zeros_like(sf_scr)
        sb_scr[...] = jnp.zeros_like(sb_scr)

    for hh in range(HG_HPS):
        _hg_head(hh, (qf_ref, zf_ref, vf_ref), (qb_ref, zb_ref, vb_ref), lb_ref, nm_ref, mk_ref,
                 (of_ref, ob_ref), (sf_scr, sb_scr))


def _hg_head(hh, ins_f, ins_b, lb_ref, nm_ref, mk_ref, o_refs, st_refs):
    c = HG_TILE
    halves = (slice(0, HG_HALF), slice(HG_HALF, c))
    lanes = slice(hh * HG_HEAD_DIM, (hh + 1) * HG_HEAD_DIM)
    ins = (ins_f, ins_b)
    g = [_hg_gates(ins[d][0][0, :, lanes], ins[d][1][0, :, lanes], ins[d][2][0, :, lanes],
                   lb_ref[d:d + 1, lanes], nm_ref, d) for d in range(2)]
    scores = [[[], []] for _ in range(2)]
    for d in range(2):
        qb = g[d]["q"].astype(BF16)
        kb = g[d]["kk"].astype(BF16)
        for i, rs in enumerate(halves):
            scores[d][i].append((_dot_nt(qb[rs], kb[rs]), 0))
        for lvl in range(HG_LEVELS - 1):
            x = g[d]["xs"][lvl].astype(BF16)
            for i, rs in enumerate(halves):
                scores[d][i].append((_dot_nt(x[rs], x[rs]), 1 + lvl))
        x = g[d]["xs"][HG_LEVELS - 1].astype(BF16)
        late, early = (1, 0) if d == 0 else (0, 1)
        scores[d][late].append((_dot_nt(x[halves[late]], x[halves[early]]), None))
    outs = []
    for d in range(2):
        vb = g[d]["v"].astype(BF16)
        late, early = (1, 0) if d == 0 else (0, 1)
        acc = []
        for i, rs in enumerate(halves):
            sm = [s.astype(BF16) * mk_ref[d, m] if m is not None else s.astype(BF16) for s, m in scores[d][i]]
            vs = [vb[rs]] * HG_LEVELS + ([vb[halves[early]]] if i == late else [])
            acc.append(jnp.dot(jnp.concatenate(sm, axis=1), jnp.concatenate(vs, axis=0),
                               preferred_element_type=F32))
        outs.append(acc)
    for d in range(2):
        st = st_refs[d][hh]
        o_refs[d][0, :, lanes] = (jnp.concatenate(outs[d], axis=0)
                                  + _dot_nt(g[d]["q_in"].astype(BF16), st.astype(BF16)))
        vt = jnp.transpose(g[d]["v"]).astype(BF16)
        st_refs[d][hh] = st * g[d]["bend"] + jnp.dot(vt, g[d]["k_out"].astype(BF16),
                                                     preferred_element_type=F32)


def _hg_scan(p3, lb, nm_bf, mk_bf):
    bsz, seq_len, _ = p3.shape
    tt = HG_TILE
    n_t = seq_len // tt
    hw = HG_HPS * HG_HEAD_DIM
    cpb = W_BRANCH // hw

    def fspec(col):
        return pl.BlockSpec((1, tt, hw), lambda b, h, k: (b, k, col * cpb + h))

    def bspec(col):
        return pl.BlockSpec((1, tt, hw), lambda b, h, k: (b, n_t - 1 - k, col * cpb + h))

    out = jax.ShapeDtypeStruct((bsz, seq_len, W_BRANCH), F32)
    return pl.pallas_call(
        _hg_kernel,
        grid=(bsz, HG_HEADS // HG_HPS, n_t),
        in_specs=[fspec(COL_C_Q), fspec(COL_C_FF), fspec(COL_C_I),
                  bspec(COL_C_Q), bspec(COL_C_FB), bspec(COL_C_I),
                  pl.BlockSpec((2, hw), lambda b, h, k: (0, h)),
                  pl.BlockSpec((2, HG_NROWS, HG_TILE), lambda b, h, k: (0, 0, 0)),
                  pl.BlockSpec((2, HG_LEVELS, HG_HALF, HG_HALF), lambda b, h, k: (0, 0, 0, 0))],
        out_specs=[pl.BlockSpec((1, tt, hw), lambda b, h, k: (b, k, h)),
                   pl.BlockSpec((1, tt, hw), lambda b, h, k: (b, n_t - 1 - k, h))],
        out_shape=[out, out],
        scratch_shapes=[pltpu.VMEM((HG_HPS, HG_HEAD_DIM, HG_HEAD_DIM), F32),
                        pltpu.VMEM((HG_HPS, HG_HEAD_DIM, HG_HEAD_DIM), F32)],
        compiler_params=_cparams(("parallel", "parallel", "arbitrary")),
        name="hgrn2_scan",
    )(p3, p3, p3, p3, p3, p3, lb, nm_bf, mk_bf)


def _hg_post_kernel(of_ref, ob_ref, gate_ref, g_ref, o_ref):
    for h in range(HG_HEADS):
        lanes = slice(h * HG_HEAD_DIM, (h + 1) * HG_HEAD_DIM)
        o = of_ref[0, :, lanes] + ob_ref[0, :, lanes]
        o = o * lax.rsqrt(jnp.mean(o * o, axis=-1, keepdims=True) + EPS)
        y = o * g_ref[:, lanes] * _silu(gate_ref[0, :, lanes])
        o_ref[0, :, lanes] = y.astype(o_ref.dtype)


def _hg_post(o_f, o_b, p3, norm_g, tt=512):
    bsz, seq_len, _ = p3.shape
    blk = pl.BlockSpec((1, tt, W_BRANCH), lambda b, t: (b, t, 0))
    return pl.pallas_call(
        _hg_post_kernel,
        grid=(bsz, seq_len // tt),
        in_specs=[blk, blk,
                  pl.BlockSpec((1, tt, W_BRANCH), lambda b, t: (b, t, COL_C_GATE)),
                  pl.BlockSpec((1, W_BRANCH), lambda b, t: (0, 0))],
        out_specs=blk,
        out_shape=jax.ShapeDtypeStruct((bsz, seq_len, W_BRANCH), BF16),
        compiler_params=_cparams(("parallel", "parallel")),
        name="hgrn2_post",
    )(o_f, o_b, p3, norm_g)


def _s5_prep_kernel(are_ref, aim_ref, ldt_ref, bre_ref, bim_ref, bbr_ref, bbi_ref, pwr_ref, pwi_ref):
    a_re = are_ref[...]
    a_im = aim_ref[...]
    dt = jnp.exp(ldt_ref[...])
    mag = jnp.exp(dt * a_re)
    ang = dt * a_im
    ab_re = mag * jnp.cos(ang)
    ab_im = mag * jnp.sin(ang)
    den = a_re * a_re + a_im * a_im
    x_ = ab_re - 1.0
    y_ = ab_im
    g_re = (x_ * a_re + y_ * a_im) / den
    g_im = (y_ * a_re - x_ * a_im) / den
    for c in range(S5_GROUP_CH):
        b_re = bre_ref[c]
        b_im = bim_ref[c]
        bbr_ref[c] = g_re * b_re - g_im * b_im
        bbi_ref[c] = g_re * b_im + g_im * b_re
    p_re = ab_re
    p_im = ab_im
    for i in range(S5_SUB):
        pwr_ref[i] = p_re
        pwi_ref[i] = p_im
        p_re, p_im = p_re * ab_re - p_im * ab_im, p_re * ab_im + p_im * ab_re


def _s5_prep(a_re, a_im, log_dt, b_re, b_im):
    n = 2 * S5_GROUPS
    sb = jax.ShapeDtypeStruct((S5_GROUP_CH, n, S5_STATE), F32)
    spw = jax.ShapeDtypeStruct((S5_SUB, n, S5_STATE), F32)
    return pl.pallas_call(
        _s5_prep_kernel,
        out_shape=[sb, sb, spw, spw],
        name="s5_prep",
    )(a_re.reshape(n, S5_STATE), a_im.reshape(n, S5_STATE), log_dt.reshape(n, 1),
      jnp.transpose(b_re.reshape(n, S5_STATE, S5_GROUP_CH), (2, 0, 1)),
      jnp.transpose(b_im.reshape(n, S5_STATE, S5_GROUP_CH), (2, 0, 1)))


def _s5_layout(bbr, bbi, pwr, pwi, c_re, c_im):
    eye = jnp.eye(S5_GPT, dtype=F32)

    def pw_tiles(x):
        return jnp.transpose(x.reshape(S5_SUB, 2, S5_LT, S5_SW), (1, 2, 0, 3))

    pw_t = jnp.stack([pw_tiles(pwr), pw_tiles(pwi)], axis=1)

    def b_bd(x):
        x = jnp.transpose(x, (1, 0, 2)).reshape(2, S5_LT, S5_GPT, S5_GROUP_CH, S5_STATE)
        bd = x[:, :, :, :, None, :] * eye[None, None, :, None, :, None]
        return bd.reshape(2, S5_LT, LANES, S5_SW)

    b_t = jnp.concatenate([b_bd(bbr), b_bd(bbi)], axis=-1)

    def c_bd(x):
        x = jnp.transpose(x.reshape(2, S5_LT, S5_GPT, S5_GROUP_CH, S5_STATE), (0, 1, 2, 4, 3))
        bd = x[:, :, :, :, None, :] * eye[None, None, :, None, :, None]
        return bd.reshape(2, S5_LT, S5_SW, LANES)

    pw_c = jnp.transpose(pw_t, (0, 1, 2, 4, 3))
    return pw_t, pw_c, b_t, c_bd(c_re), c_bd(-c_im)


def _s5_operators(prep, c_re, c_im):
    pw_t, pw_c, b_t, cre_t, cimn_t = _s5_layout(*prep, c_re, c_im)
    b8_t, kint_t, ca_t = _s5_block_prep(pw_t, pw_c, b_t, cre_t, cimn_t)
    return pw_t, b8_t, kint_t, ca_t


def _s5_perm():
    n = SUBLANES * SUBLANES
    r = np.arange(n)
    p = np.zeros((n, n), np.float32)
    p[r, (r % SUBLANES) * SUBLANES + r // SUBLANES] = 1.0
    return p


def _dot3(x, y):
    xh = x.astype(BF16)
    xl = (x - xh.astype(F32)).astype(BF16)
    yh = y.astype(BF16)
    yl = (y - yh.astype(F32)).astype(BF16)
    return (jnp.dot(xh, yh, preferred_element_type=F32) + jnp.dot(xh, yl, preferred_element_type=F32)
            + jnp.dot(xl, yh, preferred_element_type=F32))


def _s5_block_kernel(b_ref, cre_ref, cimn_ref, pw_ref, pwc_ref, b8_ref, kint_ref, ca_ref):
    n = S5_BLK
    for d in range(2):
        b = b_ref[d, 0]
        cre = cre_ref[d, 0]
        cimn = cimn_ref[d, 0]
        cfull = jnp.concatenate([cre, cimn], axis=0)

        def b_pow(k, d=d, b=b):
            if k == 0:
                return b
            pr = pw_ref[d, 0, 0, k - 1:k, :]
            pi = pw_ref[d, 1, 0, k - 1:k, :]
            bre, bim = b[:, :S5_SW], b[:, S5_SW:]
            return jnp.concatenate([bre * pr - bim * pi, bre * pi + bim * pr], axis=1)

        def c_pow(k, d=d, cre=cre, cimn=cimn):
            ar = pwc_ref[d, 0, 0, :, k - 1:k]
            ai = pwc_ref[d, 1, 0, :, k - 1:k]
            return jnp.concatenate([cre * ar + cimn * ai, cimn * ar - cre * ai], axis=0)

        bp = [b_pow(k) for k in range(n)]
        kk = [_dot3(bp[k], cfull).astype(BF16) for k in range(n)]
        zero = jnp.zeros((LANES, LANES), BF16)
        for s in range(n):
            after = n - 1 - s if d == 0 else s
            b8_ref[d, 0, s * LANES:(s + 1) * LANES, :] = bp[after].astype(BF16)
            upto = s + 1 if d == 0 else n - s
            ca_ref[d, 0, :, s * LANES:(s + 1) * LANES] = c_pow(upto).astype(BF16)
            for r in range(n):
                lag = r - s if d == 0 else s - r
                kint_ref[d, 0, s * LANES:(s + 1) * LANES, r * LANES:(r + 1) * LANES] = kk[lag] if lag >= 0 else zero


def _s5_block_prep(pw_t, pw_c, b_t, cre_t, cimn_t):
    w = S5_BLK * LANES
    out = jax.ShapeDtypeStruct((2, S5_LT, w, w), BF16)
    out_b8 = jax.ShapeDtypeStruct((2, S5_LT, w, 2 * S5_SW), BF16)
    out_ca = jax.ShapeDtypeStruct((2, S5_LT, 2 * S5_SW, w), BF16)
    return pl.pallas_call(
        _s5_block_kernel,
        grid=(S5_LT,),
        in_specs=[pl.BlockSpec((2, 1, LANES, 2 * S5_SW), lambda l: (0, l, 0, 0)),
                  pl.BlockSpec((2, 1, S5_SW, LANES), lambda l: (0, l, 0, 0)),
                  pl.BlockSpec((2, 1, S5_SW, LANES), lambda l: (0, l, 0, 0)),
                  pl.BlockSpec((2, 2, 1, S5_SUB, S5_SW), lambda l: (0, 0, l, 0, 0)),
                  pl.BlockSpec((2, 2, 1, S5_SW, S5_SUB), lambda l: (0, 0, l, 0, 0))],
        out_specs=[pl.BlockSpec((2, 1, w, 2 * S5_SW), lambda l: (0, l, 0, 0)),
                   pl.BlockSpec((2, 1, w, w), lambda l: (0, l, 0, 0)),
                   pl.BlockSpec((2, 1, 2 * S5_SW, w), lambda l: (0, l, 0, 0))],
        out_shape=[out_b8, out, out_ca],
        compiler_params=_cparams(("parallel",)),
        name="s5_block_prep",
    )(b_t, cre_t, cimn_t, pw_t, pw_c)


def _vreg_grid_t(x):
    g = SUBLANES
    return jnp.concatenate(
        [jnp.concatenate([x[q * g:(q + 1) * g, p * LANES:(p + 1) * LANES] for q in range(g)], axis=1)
         for p in range(g)], axis=0)


def _s5_to_wide(x):
    g = SUBLANES
    return jnp.concatenate(
        [jnp.concatenate([x[(p * g + q) * g:(p * g + q + 1) * g, :] for q in range(g)], axis=1) for p in range(g)],
        axis=0)


def _s5_from_wide(x):
    g = SUBLANES
    return jnp.concatenate([x[p * g:(p + 1) * g, q * LANES:(q + 1) * LANES] for p in range(g) for q in range(g)],
                           axis=0)


def _s5_kernel(u_ref, pw_ref, b8_ref, kint_ref, ca_ref, d_ref, pm_ref, o_ref,
               drv_scr, y8_scr, hin_scr, *, seq_len):
    n_g = seq_len // (S5_TILE * S5_GROUP)
    nb = S5_SUB // S5_BLK
    rpt = SUBLANES * nb
    w = SUBLANES * LANES
    o_ref[0] = u_ref[0] * d_ref[...]
    zero_row = jnp.zeros((1, S5_SW), F32)
    zero_blk = jnp.zeros((SUBLANES, S5_SW), F32)

    def power(d, k):
        return pw_ref[d, 0, 0, k - 1:k, :], pw_ref[d, 1, 0, k - 1:k, :]

    def tile_rows(kg, i, d):
        t = kg * S5_GROUP + i
        kt = t if d == 0 else n_g * S5_GROUP - 1 - t
        return pl.ds(pl.multiple_of(kt * S5_TILE, S5_TILE), S5_TILE)

    def scan_tile(d, i, carry_re, carry_im):
        order = list(range(nb)) if d == 0 else list(range(nb - 1, -1, -1))
        ar, ai = (jnp.broadcast_to(x, (SUBLANES, S5_SW)) for x in power(d, S5_BLK))
        hr, hi = zero_blk, zero_blk
        loc = {}
        for m in order:
            rows = slice(i * rpt + m * SUBLANES, i * rpt + (m + 1) * SUBLANES)
            hr, hi = (ar * hr - ai * hi + drv_scr[d, rows, :S5_SW], ar * hi + ai * hr + drv_scr[d, rows, S5_SW:])
            loc[m] = (hr, hi)
        asr, asi = power(d, S5_SUB)
        c_re, c_im = carry_re, carry_im
        ent_re, ent_im = [None] * SUBLANES, [None] * SUBLANES
        for j in (range(SUBLANES) if d == 0 else range(SUBLANES - 1, -1, -1)):
            ent_re[j], ent_im[j] = c_re, c_im
            er, ei = hr[j:j + 1, :], hi[j:j + 1, :]
            c_re, c_im = er + asr * c_re - asi * c_im, ei + asr * c_im + asi * c_re
        ent_re = jnp.concatenate(ent_re, axis=0)
        ent_im = jnp.concatenate(ent_im, axis=0)
        prev_re, prev_im = {order[0]: ent_re}, {order[0]: ent_im}
        for idx, m in enumerate(order[:-1]):
            pr, pi = (jnp.broadcast_to(x, (SUBLANES, S5_SW)) for x in power(d, S5_BLK * (idx + 1)))
            prev_re[order[idx + 1]] = loc[m][0] + pr * ent_re - pi * ent_im
            prev_im[order[idx + 1]] = loc[m][1] + pr * ent_im + pi * ent_re
        h_in = jnp.concatenate([jnp.concatenate([prev_re[m] for m in range(nb)], axis=0),
                                jnp.concatenate([prev_im[m] for m in range(nb)], axis=0)], axis=1)
        hin_scr[d, i * rpt:(i + 1) * rpt, :] = h_in.astype(BF16)
        return c_re, c_im

    def group_body(kg, carry):
        for d in range(2):
            wide = jnp.concatenate([_s5_to_wide(u_ref[0, tile_rows(kg, i, d), :]) for i in range(S5_GROUP)], axis=1)
            perm = jnp.dot(pm_ref[...], wide.astype(BF16), preferred_element_type=F32)
            u8 = jnp.concatenate([_vreg_grid_t(perm[:, i * w:(i + 1) * w]) for i in range(S5_GROUP)],
                                 axis=0).astype(BF16)
            drv_scr[d] = jnp.dot(u8, b8_ref[d, 0], preferred_element_type=F32)
            y8_scr[d] = jnp.dot(u8, kint_ref[d, 0], preferred_element_type=F32)
        new_carry = []
        for d in range(2):
            c_re, c_im = carry[2 * d], carry[2 * d + 1]
            for i in range(S5_GROUP):
                c_re, c_im = scan_tile(d, i, c_re, c_im)
            new_carry += [c_re, c_im]
        for d in range(2):
            y = y8_scr[d] + jnp.dot(hin_scr[d], ca_ref[d, 0], preferred_element_type=F32)
            yt = jnp.concatenate([_vreg_grid_t(y[i * rpt:(i + 1) * rpt]) for i in range(S5_GROUP)], axis=1)
            y_hi = yt.astype(BF16)
            y_lo = (yt - y_hi.astype(F32)).astype(BF16)
            yn = (jnp.dot(pm_ref[...], y_hi, preferred_element_type=F32)
                  + jnp.dot(pm_ref[...], y_lo, preferred_element_type=F32))
            for i in range(S5_GROUP):
                rows = tile_rows(kg, i, d)
                o_ref[0, rows, :] = o_ref[0, rows, :] + _s5_from_wide(yn[:, i * w:(i + 1) * w])
        return tuple(new_carry)

    lax.fori_loop(0, n_g, group_body, (zero_row,) * 4)


def _s5_scan(p3, pw_t, b8_t, kint_t, ca_t, d_skip, perm):
    bsz, seq_len, _ = p3.shape
    cpb = W_BRANCH // LANES
    w = S5_BLK * LANES
    rows = S5_GROUP * SUBLANES * (S5_SUB // S5_BLK)
    return pl.pallas_call(
        functools.partial(_s5_kernel, seq_len=seq_len),
        grid=(S5_LT, bsz),
        in_specs=[pl.BlockSpec((1, seq_len, LANES), lambda l, b: (b, 0, COL_D_IN * cpb + l)),
                  pl.BlockSpec((2, 2, 1, S5_SUB, S5_SW), lambda l, b: (0, 0, l, 0, 0)),
                  pl.BlockSpec((2, 1, w, 2 * S5_SW), lambda l, b: (0, l, 0, 0)),
                  pl.BlockSpec((2, 1, w, w), lambda l, b: (0, l, 0, 0)),
                  pl.BlockSpec((2, 1, 2 * S5_SW, w), lambda l, b: (0, l, 0, 0)),
                  pl.BlockSpec((1, LANES), lambda l, b: (0, l)),
                  pl.BlockSpec((SUBLANES * SUBLANES, SUBLANES * SUBLANES), lambda l, b: (0, 0))],
        out_specs=pl.BlockSpec((1, seq_len, LANES), lambda l, b: (b, 0, l)),
        out_shape=jax.ShapeDtypeStruct((bsz, seq_len, W_BRANCH), F32),
        scratch_shapes=[pltpu.VMEM((2, rows, 2 * S5_SW), F32),
                        pltpu.VMEM((2, rows, w), F32),
                        pltpu.VMEM((2, rows, 2 * S5_SW), BF16)],
        compiler_params=_cparams(("parallel", "parallel")),
        name="s5_scan",
    )(p3, pw_t, b8_t, kint_t, ca_t, d_skip, perm)


def _s5_post_kernel(y_ref, gate_ref, w_ref, b_ref, o_ref):
    y = y_ref[0]
    c0 = math.sqrt(2.0 / math.pi)
    z = 0.5 * y * (1.0 + jnp.tanh(c0 * (y + 0.044715 * (y * y * y))))
    lin = jnp.dot(z.astype(BF16), w_ref[...], preferred_element_type=F32) + b_ref[...]
    out = z * _sigmoid(lin) * _silu(gate_ref[0])
    o_ref[0] = out.astype(o_ref.dtype)


def _s5_post(y, p3, glu_w_bf, glu_b, tt=512):
    bsz, seq_len, _ = p3.shape
    blk = pl.BlockSpec((1, tt, W_BRANCH), lambda b, t: (b, t, 0))
    return pl.pallas_call(
        _s5_post_kernel,
        grid=(bsz, seq_len // tt),
        in_specs=[blk,
                  pl.BlockSpec((1, tt, W_BRANCH), lambda b, t: (b, t, COL_D_GATE)),
                  pl.BlockSpec((W_BRANCH, W_BRANCH), lambda b, t: (0, 0)),
                  pl.BlockSpec((1, W_BRANCH), lambda b, t: (0, 0))],
        out_specs=blk,
        out_shape=jax.ShapeDtypeStruct((bsz, seq_len, W_BRANCH), BF16),
        compiler_params=_cparams(("parallel", "parallel")),
        name="s5_post",
    )(y, p3, glu_w_bf, glu_b)


def _merge_kernel(ya_ref, yb_ref, yc_ref, yd_ref, r0_ref, r1_ref, r2_ref, r3_ref, w_ref, o_ref):
    acc = None
    for i, (y_ref, r_ref) in enumerate(((ya_ref, r0_ref), (yb_ref, r1_ref), (yc_ref, r2_ref), (yd_ref, r3_ref))):
        term = _sigmoid(r_ref[...]) * jnp.dot(y_ref[...], w_ref[i], preferred_element_type=F32)
        acc = term if acc is None else acc + term
    o_ref[...] = acc.astype(o_ref.dtype)


def _merge(ys, p2, w_stack_bf, tm=512, tn=1024):
    t = p2.shape[0]
    npb = D_MODEL // tn
    roff = COL_R * W_BRANCH // tn
    yspec = pl.BlockSpec((tm, W_BRANCH), lambda j, i: (i, 0))

    def rspec(br):
        return pl.BlockSpec((tm, tn), lambda j, i: (i, roff + br * npb + j))

    return pl.pallas_call(
        _merge_kernel,
        grid=(npb, t // tm),
        in_specs=[yspec, yspec, yspec, yspec, rspec(0), rspec(1), rspec(2), rspec(3),
                  pl.BlockSpec((N_BRANCH, W_BRANCH, tn), lambda j, i: (0, 0, j))],
        out_specs=pl.BlockSpec((tm, tn), lambda j, i: (i, j)),
        out_shape=jax.ShapeDtypeStruct((t, D_MODEL), BF16),
        compiler_params=_cparams(("parallel", "parallel")),
        name="merge",
    )(*ys, p2, p2, p2, p2, w_stack_bf)


def _out_proj_kernel(x_ref, m_ref, w_ref, g_ref, o_ref, *, final_norm):
    y = x_ref[...] + jnp.dot(m_ref[...], w_ref[...], preferred_element_type=F32)
    if final_norm:
        y = y * lax.rsqrt(jnp.mean(y * y, axis=-1, keepdims=True) + EPS) * g_ref[...]
    o_ref[...] = y


def _out_proj(x2, m, w_bf, final_g, final_norm, tm=512):
    t = x2.shape[0]
    blk = pl.BlockSpec((tm, D_MODEL), lambda i: (i, 0))
    return pl.pallas_call(
        functools.partial(_out_proj_kernel, final_norm=final_norm),
        grid=(t // tm,),
        in_specs=[blk, blk,
                  pl.BlockSpec((D_MODEL, D_MODEL), lambda i: (0, 0)),
                  pl.BlockSpec((1, D_MODEL), lambda i: (0, 0))],
        out_specs=blk,
        out_shape=jax.ShapeDtypeStruct((t, D_MODEL), F32),
        compiler_params=_cparams(("parallel",)),
        name="out_proj",
    )(x2, m, w_bf, final_g)


def _lb_kernel(x_ref, o_ref):
    x = x_ref[...]
    e = jnp.exp(x - jnp.max(x, axis=0, keepdims=True))
    sm = e / jnp.sum(e, axis=0, keepdims=True)
    run = jnp.zeros_like(sm[0])
    for l in range(DEPTH):
        run = run + sm[l]
        o_ref[l] = run - sm[0]


def _lower_bounds(hg_lb):
    return pl.pallas_call(
        _lb_kernel,
        out_shape=jax.ShapeDtypeStruct(hg_lb.shape, F32),
        name="hg_lower_bounds",
    )(hg_lb)


def _layer(x, lw, final_g, final_norm):
    bsz, seq_len, _ = x.shape
    x2 = x.reshape(bsz * seq_len, D_MODEL)
    p2 = _in_proj(x2, lw["norm_g"], lw["w_in"])
    p3 = p2.reshape(bsz, seq_len, N_IN)
    ya = _conv_branch(p3, lw["conv_w"], lw["conv_b"], lw["conv_ln_g"], lw["conv_ln_b"])
    yb = _pool_branch(p3, lw["pool_w"], lw["pool_scale"])
    o_f, o_b = _hg_scan(p3, lw["lb"], lw["hg_nm"], lw["hg_mk"])
    yc = _hg_post(o_f, o_b, p3, lw["hg_norm_g"])
    y5 = _s5_scan(p3, *lw["s5"], lw["s5_d"], lw["s5_perm"])
    yd = _s5_post(y5, p3, lw["s5_glu_w"], lw["s5_glu_b"])
    ys = [y.reshape(bsz * seq_len, W_BRANCH) for y in (ya, yb, yc, yd)]
    m = _merge(ys, p2, lw["w_branch_out"])
    out = _out_proj(x2, m, lw["w_out"], final_g, final_norm)
    return out.reshape(bsz, seq_len, D_MODEL)


def kernel(x_prompt, x_sample, norm_g, w_in, conv_w, conv_b, conv_ln_g, conv_ln_b, w_a_out, pool_w, pool_scale, w_b_out, hg_lb, hg_norm_g, w_c_out, s5_a_re, s5_a_im, s5_log_dt, s5_b_re, s5_b_im, s5_c_re, s5_c_im, s5_d, s5_glu_w, s5_glu_b, w_d_out, w_out, final_g):
    lb_all = _lower_bounds(hg_lb)
    nm, mk = _hg_tables()
    nm_bf = jnp.asarray(nm, BF16)
    mk_bf = jnp.asarray(mk, BF16)
    perm_bf = jnp.asarray(_s5_perm(), BF16)
    layers = []
    for l in range(DEPTH):
        prep = _s5_prep(s5_a_re[l], s5_a_im[l], s5_log_dt[l], s5_b_re[l], s5_b_im[l])
        layers.append(dict(
            norm_g=norm_g[l][None], w_in=w_in[l].astype(BF16),
            conv_w=conv_w[l], conv_b=conv_b[l][None], conv_ln_g=conv_ln_g[l][None], conv_ln_b=conv_ln_b[l][None],
            pool_w=pool_w[l].astype(BF16), pool_scale=pool_scale[l][None],
            lb=lb_all[l], hg_nm=nm_bf, hg_mk=mk_bf, hg_norm_g=hg_norm_g[l][None],
            s5=_s5_operators(prep, s5_c_re[l], s5_c_im[l]), s5_d=s5_d[l][None],
            s5_perm=perm_bf,
            s5_glu_w=s5_glu_w[l].astype(BF16), s5_glu_b=s5_glu_b[l][None],
            w_branch_out=jnp.stack([w_a_out[l], w_b_out[l], w_c_out[l], w_d_out[l]]).astype(BF16),
            w_out=w_out[l].astype(BF16)))
    fg = final_g[None]

    def run(x):
        for l in range(DEPTH):
            x = _layer(x, layers[l], fg, l == DEPTH - 1)
        return x

    return (run(x_prompt), run(x_sample))
```

```python
import functools
import math

import numpy as np
import jax
import jax.numpy as jnp
from jax import lax
from jax.experimental import pallas as pl
from jax.experimental.pallas import tpu as pltpu

F32 = jnp.float32
BF16 = jnp.bfloat16

D_MODEL = 2048
DEPTH = 2
W_BRANCH = 1024
N_BRANCH = 4
N_IN = 12 * W_BRANCH + N_BRANCH * D_MODEL
CONV_WIDTH = 31
CONV_PAD = CONV_WIDTH // 2
POOL_WINDOWS = (2, 4, 8, 16)
POOL_GROUP = W_BRANCH // len(POOL_WINDOWS)
HG_HEAD_DIM = 128
HG_HEADS = W_BRANCH // HG_HEAD_DIM
S5_GROUP_CH = 16
S5_GROUPS = W_BRANCH // S5_GROUP_CH
S5_STATE = 64
EPS = 1e-6

LANES = 128
SUBLANES = 8
VMEM_LIMIT = 52 * 1024 * 1024

COL_A_VAL, COL_A_GLU, COL_A_GATE, COL_B_IN, COL_B_GATE = 0, 1, 2, 3, 4
COL_C_Q, COL_C_FF, COL_C_FB, COL_C_I, COL_C_GATE, COL_D_IN, COL_D_GATE = 5, 6, 7, 8, 9, 10, 11
COL_R = 12

HALO = 16
CONV_LANES = 256
HG_TILE = 256
HG_HALF = HG_TILE // 2
HG_HPS = 8
HG_LEVELS = 8
HG_NROWS = 4 * HG_TILE
S5_TILE = 512
S5_SUB = S5_TILE // SUBLANES
S5_BLK = 8
S5_GROUP = 4
S5_LT = W_BRANCH // LANES
S5_GPT = LANES // S5_GROUP_CH
S5_SW = S5_GPT * S5_STATE


def _sigmoid(x):
    return 1.0 / (1.0 + jnp.exp(-x))


def _silu(x):
    return x * _sigmoid(x)


def _cparams(sem):
    return pltpu.CompilerParams(dimension_semantics=sem, vmem_limit_bytes=VMEM_LIMIT)


def _in_proj_kernel(x_ref, g_ref, w_ref, o_ref, r_ref, h_ref, *, n_main):
    j = pl.program_id(1)

    @pl.when(j == 0)
    def _():
        x = x_ref[...]
        ms = jnp.mean(x * x, axis=-1, keepdims=True)
        h_ref[...] = (x * lax.rsqrt(ms + EPS) * g_ref[...]).astype(BF16)

    @pl.when(j < n_main)
    def _():
        o_ref[...] = jnp.dot(h_ref[...], w_ref[...], preferred_element_type=F32)

    @pl.when(j >= n_main)
    def _():
        r_ref[...] = jnp.dot(h_ref[...], w_ref[...], preferred_element_type=F32).astype(BF16)


def _in_proj(x2, g, w_bf, tm=1024, tn=1024):
    t = x2.shape[0]
    n_main = COL_R * W_BRANCH // tn
    return pl.pallas_call(
        functools.partial(_in_proj_kernel, n_main=n_main),
        grid=(t // tm, N_IN // tn),
        in_specs=[pl.BlockSpec((tm, D_MODEL), lambda i, j: (i, 0)),
                  pl.BlockSpec((1, D_MODEL), lambda i, j: (0, 0)),
                  pl.BlockSpec((D_MODEL, tn), lambda i, j: (0, j))],
        out_specs=[pl.BlockSpec((tm, tn), lambda i, j: (i, jnp.minimum(j, n_main - 1))),
                   pl.BlockSpec((tm, tn), lambda i, j: (i, jnp.maximum(j - n_main, 0)))],
        out_shape=[jax.ShapeDtypeStruct((t, COL_R * W_BRANCH), F32),
                   jax.ShapeDtypeStruct((t, N_BRANCH * D_MODEL), BF16)],
        scratch_shapes=[pltpu.VMEM((tm, D_MODEL), BF16)],
        compiler_params=_cparams(("parallel", "arbitrary")),
        name="in_proj",
    )(x2, g, w_bf)


def _conv_kernel(main_ref, gate_ref, prev_ref, next_ref, cw_ref, cb_ref, lg_ref, lb_ref, o_ref,
                 us_scr, wb_scr, *, tt, rc):
    t = pl.program_id(1)
    n_t = pl.num_programs(1)

    def glu(ref):
        return ref[0, :, :W_BRANCH] * _sigmoid(ref[0, :, W_BRANCH:])

    us_scr[0, HALO:HALO + tt, :] = glu(main_ref)
    us_scr[0, 0:HALO, :] = jnp.where(t > 0, glu(prev_ref), 0.0)
    us_scr[0, HALO + tt:2 * HALO + tt, :] = jnp.where(t < n_t - 1, glu(next_ref), 0.0)
    n = tt + 2 * HALO - SUBLANES
    for s in range(1, SUBLANES):
        us_scr[s, 0:n, :] = us_scr[0, s:s + n, :]
    for j in range(CONV_WIDTH):
        wb_scr[j] = jnp.broadcast_to(cw_ref[j:j + 1, :], (SUBLANES, W_BRANCH))

    def body(c, carry):
        r0 = pl.multiple_of(c * rc, rc)
        parts = []
        for lc in range(W_BRANCH // CONV_LANES):
            lanes = slice(lc * CONV_LANES, (lc + 1) * CONV_LANES)
            acc = jnp.zeros((rc // SUBLANES, SUBLANES, CONV_LANES), F32)
            for j in range(CONV_WIDTH):
                off = HALO - CONV_PAD + j
                s = off % SUBLANES
                tap = us_scr[s, pl.ds(r0 + (off - s), rc), lanes]
                acc = acc + tap.reshape(rc // SUBLANES, SUBLANES, CONV_LANES) * wb_scr[j, :, lanes]
            parts.append(acc.reshape(rc, CONV_LANES))
        y = jnp.concatenate(parts, axis=-1) + cb_ref[...]
        mu = jnp.mean(y, axis=-1, keepdims=True)
        yc = y - mu
        var = jnp.mean(yc * yc, axis=-1, keepdims=True)
        yn = yc * lax.rsqrt(var + EPS) * lg_ref[...] + lb_ref[...]
        out = _silu(yn) * _silu(gate_ref[0, pl.ds(r0, rc), :])
        o_ref[0, pl.ds(r0, rc), :] = out.astype(o_ref.dtype)
        return carry

    lax.fori_loop(0, tt // rc, body, 0)


def _halo_specs(tt, width, col_blk, seq_len):
    per = tt // HALO
    last = seq_len // HALO - 1
    prev = pl.BlockSpec((1, HALO, width), lambda b, t: (b, jnp.maximum(t * per - 1, 0), col_blk))
    nxt = pl.BlockSpec((1, HALO, width), lambda b, t: (b, jnp.minimum((t + 1) * per, last), col_blk))
    return prev, nxt


def _conv_branch(p3, cw, cb, lg, lb, tt=512, rc=32):
    bsz, seq_len, _ = p3.shape
    prev, nxt = _halo_specs(tt, 2 * W_BRANCH, 0, seq_len)
    vec = pl.BlockSpec((1, W_BRANCH), lambda b, t: (0, 0))
    return pl.pallas_call(
        functools.partial(_conv_kernel, tt=tt, rc=rc),
        grid=(bsz, seq_len // tt),
        in_specs=[pl.BlockSpec((1, tt, 2 * W_BRANCH), lambda b, t: (b, t, 0)),
                  pl.BlockSpec((1, tt, W_BRANCH), lambda b, t: (b, t, COL_A_GATE)),
                  prev, nxt,
                  pl.BlockSpec((CONV_WIDTH, W_BRANCH), lambda b, t: (0, 0)),
                  vec, vec, vec],
        out_specs=pl.BlockSpec((1, tt, W_BRANCH), lambda b, t: (b, t, 0)),
        out_shape=jax.ShapeDtypeStruct((bsz, seq_len, W_BRANCH), BF16),
        scratch_shapes=[pltpu.VMEM((SUBLANES, tt + 2 * HALO, W_BRANCH), F32),
                        pltpu.VMEM((CONV_WIDTH, SUBLANES, W_BRANCH), F32)],
        compiler_params=_cparams(("parallel", "parallel")),
        name="conv_branch",
    )(p3, p3, p3, p3, cw, cb, lg, lb)


def _pool_kernel(main_ref, gate_ref, prev_ref, next_ref, pw_ref, ps_ref, o_ref, e_scr, s_scr,
                 *, tt, seq_len):
    t = pl.program_id(1)
    n_t = pl.num_programs(1)
    rows = tt + 2 * HALO
    e_scr[HALO:HALO + tt, :] = main_ref[0]
    e_scr[0:HALO, :] = jnp.where(t > 0, prev_ref[0], 0.0)
    e_scr[HALO + tt:rows, :] = jnp.where(t < n_t - 1, next_ref[0], 0.0)

    tpos = t * tt + lax.broadcasted_iota(jnp.int32, (tt, 1), 0)
    for g, win in enumerate(POOL_WINDOWS):
        lanes = slice(g * POOL_GROUP, (g + 1) * POOL_GROUP)
        n = rows - 2
        s_scr[1:1 + n, lanes] = e_scr[1:1 + n, lanes] + e_scr[0:n, lanes]
        half = 1
        while 2 * half < win:
            n = rows - 2 * half - 2 * half
            lo = 2 * half
            a = s_scr[lo + half:lo + half + n, lanes]
            b = s_scr[lo - half:lo - half + n, lanes]
            s_scr[lo:lo + n, lanes] = a + b
            half *= 2
        left = win // 2
        right = win - 1 - left
        cnt = (jnp.minimum(tpos + right, seq_len - 1) + 1 - jnp.maximum(tpos - left, 0)).astype(F32)
        u = e_scr[HALO:HALO + tt, lanes]
        d = s_scr[HALO:HALO + tt, lanes] / cnt - u
        y = jnp.dot(d.astype(BF16), pw_ref[g], preferred_element_type=F32)
        y = y * ps_ref[:, lanes] * _silu(gate_ref[0, :, lanes])
        o_ref[0, :, lanes] = y.astype(o_ref.dtype)


def _pool_branch(p3, pw_bf, ps, tt=512):
    bsz, seq_len, _ = p3.shape
    prev, nxt = _halo_specs(tt, W_BRANCH, COL_B_IN, seq_len)
    return pl.pallas_call(
        functools.partial(_pool_kernel, tt=tt, seq_len=seq_len),
        grid=(bsz, seq_len // tt),
        in_specs=[pl.BlockSpec((1, tt, W_BRANCH), lambda b, t: (b, t, COL_B_IN)),
                  pl.BlockSpec((1, tt, W_BRANCH), lambda b, t: (b, t, COL_B_GATE)),
                  prev, nxt,
                  pl.BlockSpec((len(POOL_WINDOWS), POOL_GROUP, POOL_GROUP), lambda b, t: (0, 0, 0)),
                  pl.BlockSpec((1, W_BRANCH), lambda b, t: (0, 0))],
        out_specs=pl.BlockSpec((1, tt, W_BRANCH), lambda b, t: (b, t, 0)),
        out_shape=jax.ShapeDtypeStruct((bsz, seq_len, W_BRANCH), BF16),
        scratch_shapes=[pltpu.VMEM((tt + 2 * HALO, W_BRANCH), F32),
                        pltpu.VMEM((tt + 2 * HALO, W_BRANCH), F32)],
        compiler_params=_cparams(("parallel", "parallel")),
        name="pool_branch",
    )(p3, p3, p3, p3, pw_bf, ps)


def _hg_tables():
    c = HG_TILE
    t = np.arange(c)
    rr = np.arange(c)[None, :]
    vstart = (t // SUBLANES) * SUBLANES
    pref = [t, vstart + 7, vstart + 3, vstart + np.where(t % SUBLANES < 4, 1, 5)]
    nmat = np.stack([rr <= p[:, None] for p in pref]).astype(np.float32)
    masks = np.zeros((HG_LEVELS, HG_HALF, HG_HALF), np.float32)
    masks[0] = np.eye(HG_HALF)
    th = np.arange(HG_HALF)
    for lvl in range(HG_LEVELS - 1):
        half = 1 << lvl
        bh = th // (2 * half)
        lh = (th % (2 * half)) >= half
        masks[1 + lvl] = (bh[:, None] == bh[None, :]) & lh[:, None] & (~lh)[None, :]
    nm = np.stack([nmat, nmat[:, ::-1, ::-1]]).reshape(2, HG_NROWS, c)
    mk = np.stack([masks, masks[:, ::-1, ::-1]])
    return nm, mk


def _dot_nt(a, b):
    return lax.dot_general(a, b, (((1,), (1,)), ((), ())), preferred_element_type=F32)


def _hg_ref_rows(lvl, d):
    out = []
    for m in range(HG_TILE // SUBLANES):
        block = 2 << lvl
        mid = (m * SUBLANES // block) * block + block // 2
        out.append(mid // SUBLANES - 1 if d == 0 else mid // SUBLANES)
    return out


def _hg_gates(q, z, v, lbv, nm_ref, d):
    c = HG_TILE
    sg = _sigmoid(z)
    f = lbv + (1.0 - lbv) * sg
    lf = jnp.log2(f)
    kk = (1.0 - lbv) * (1.0 - sg)
    hi = lf.astype(BF16)
    lo = (lf - hi.astype(F32)).astype(BF16)
    tab = jnp.dot(nm_ref[d], jnp.concatenate([hi, lo], axis=1), preferred_element_type=F32)
    tab = tab[:, :HG_HEAD_DIM] + tab[:, HG_HEAD_DIM:]
    b, b_grp, b_l2, b_l1 = (tab[i * c:(i + 1) * c] for i in range(4))
    b_end = b_grp[c - 1:c] if d == 0 else b_grp[0:1]
    n_g = c // SUBLANES
    sub = lax.broadcasted_iota(jnp.int32, (1, SUBLANES, HG_HEAD_DIM), 1)
    q3 = q.reshape(n_g, SUBLANES, HG_HEAD_DIM)
    k3 = kk.reshape(n_g, SUBLANES, HG_HEAD_DIM)
    xs = []
    for lvl in range(HG_LEVELS):
        if (1 << lvl) < SUBLANES:
            later = (jnp.right_shift(sub, lvl) & 1) == (1 - d)
            base = jnp.where(later, q3, k3).reshape(c, HG_HEAD_DIM)
        else:
            pick = [((m * SUBLANES) >> lvl) & 1 == (1 - d) for m in range(n_g)]
            base = jnp.concatenate([(q3 if p else k3)[m] for m, p in enumerate(pick)], axis=0)
        if lvl == 0:
            later0 = jnp.broadcast_to(later, (n_g, SUBLANES, HG_HEAD_DIM)).reshape(c, HG_HEAD_DIM)
            xs.append(jnp.where(later0, base * f, base))
            continue
        if lvl == 1:
            ref = b_l1
        elif lvl == 2:
            ref = b_l2
        else:
            ref = jnp.concatenate([b_grp[g * SUBLANES:(g + 1) * SUBLANES] for g in _hg_ref_rows(lvl, d)], axis=0)
        neg_abs = pltpu.bitcast(pltpu.bitcast(b - ref, jnp.uint32) | jnp.uint32(0x80000000), F32)
        xs.append(base * jnp.exp2(neg_abs))
    return dict(q=q, kk=kk, v=v, xs=xs, q_in=q * jnp.exp2(b), k_out=kk * jnp.exp2(b_end - b),
                bend=jnp.exp2(b_end))


def _hg_kernel(qf_ref, zf_ref, vf_ref, qb_ref, zb_ref, vb_ref, lb_ref, nm_ref, mk_ref,
               of_ref, ob_ref, sf_scr, sb_scr):
    @pl.when(pl.program_id(2) == 0)
    def _():
        sf_scr[...] = jnp.zeros_like(sf_scr)
        sb_scr[...] = jnp.zeros_like(sb_scr)

    for hh in range(HG_HPS):
        _hg_head(hh, (qf_ref, zf_ref, vf_ref), (qb_ref, zb_ref, vb_ref), lb_ref, nm_ref, mk_ref,
                 (of_ref, ob_ref), (sf_scr, sb_scr))


def _hg_head(hh, ins_f, ins_b, lb_ref, nm_ref, mk_ref, o_refs, st_refs):
    c = HG_TILE
    halves = (slice(0, HG_HALF), slice(HG_HALF, c))
    lanes = slice(hh * HG_HEAD_DIM, (hh + 1) * HG_HEAD_DIM)
    ins = (ins_f, ins_b)
    g = [_hg_gates(ins[d][0][0, :, lanes], ins[d][1][0, :, lanes], ins[d][2][0, :, lanes],
                   lb_ref[d:d + 1, lanes], nm_ref, d) for d in range(2)]
    scores = [[[], []] for _ in range(2)]
    for d in range(2):
        qb = g[d]["q"].astype(BF16)
        kb = g[d]["kk"].astype(BF16)
        for i, rs in enumerate(halves):
            scores[d][i].append((_dot_nt(qb[rs], kb[rs]), 0))
        for lvl in range(HG_LEVELS - 1):
            x = g[d]["xs"][lvl].astype(BF16)
            for i, rs in enumerate(halves):
                scores[d][i].append((_dot_nt(x[rs], x[rs]), 1 + lvl))
        x = g[d]["xs"][HG_LEVELS - 1].astype(BF16)
        late, early = (1, 0) if d == 0 else (0, 1)
        scores[d][late].append((_dot_nt(x[halves[late]], x[halves[early]]), None))
    outs = []
    for d in range(2):
        vb = g[d]["v"].astype(BF16)
        late, early = (1, 0) if d == 0 else (0, 1)
        acc = []
        for i, rs in enumerate(halves):
            sm = [s.astype(BF16) * mk_ref[d, m] if m is not None else s.astype(BF16) for s, m in scores[d][i]]
            vs = [vb[rs]] * HG_LEVELS + ([vb[halves[early]]] if i == late else [])
            acc.append(jnp.dot(jnp.concatenate(sm, axis=1), jnp.concatenate(vs, axis=0),
                               preferred_element_type=F32))
        outs.append(acc)
    for d in range(2):
        st = st_refs[d][hh]
        o_refs[d][0, :, lanes] = (jnp.concatenate(outs[d], axis=0)
                                  + _dot_nt(g[d]["q_in"].astype(BF16), st.astype(BF16))).astype(o_refs[d].dtype)
        vt = jnp.transpose(g[d]["v"]).astype(BF16)
        st_refs[d][hh] = st * g[d]["bend"] + jnp.dot(vt, g[d]["k_out"].astype(BF16),
                                                     preferred_element_type=F32)


def _hg_scan(p3, lb, nm_bf, mk_bf):
    bsz, seq_len, _ = p3.shape
    tt = HG_TILE
    n_t = seq_len // tt
    hw = HG_HPS * HG_HEAD_DIM
    cpb = W_BRANCH // hw

    def fspec(col):
        return pl.BlockSpec((1, tt, hw), lambda b, h, k: (b, k, col * cpb + h))

    def bspec(col):
        return pl.BlockSpec((1, tt, hw), lambda b, h, k: (b, n_t - 1 - k, col * cpb + h))

    out = jax.ShapeDtypeStruct((bsz, seq_len, W_BRANCH), BF16)
    return pl.pallas_call(
        _hg_kernel,
        grid=(bsz, HG_HEADS // HG_HPS, n_t),
        in_specs=[fspec(COL_C_Q), fspec(COL_C_FF), fspec(COL_C_I),
                  bspec(COL_C_Q), bspec(COL_C_FB), bspec(COL_C_I),
                  pl.BlockSpec((2, hw), lambda b, h, k: (0, h)),
                  pl.BlockSpec((2, HG_NROWS, HG_TILE), lambda b, h, k: (0, 0, 0)),
                  pl.BlockSpec((2, HG_LEVELS, HG_HALF, HG_HALF), lambda b, h, k: (0, 0, 0, 0))],
        out_specs=[pl.BlockSpec((1, tt, hw), lambda b, h, k: (b, k, h)),
                   pl.BlockSpec((1, tt, hw), lambda b, h, k: (b, n_t - 1 - k, h))],
        out_shape=[out, out],
        scratch_shapes=[pltpu.VMEM((HG_HPS, HG_HEAD_DIM, HG_HEAD_DIM), F32),
                        pltpu.VMEM((HG_HPS, HG_HEAD_DIM, HG_HEAD_DIM), F32)],
        compiler_params=_cparams(("parallel", "parallel", "arbitrary")),
        name="hgrn2_scan",
    )(p3, p3, p3, p3, p3, p3, lb, nm_bf, mk_bf)


def _hg_post_kernel(of_ref, ob_ref, gate_ref, g_ref, o_ref):
    for h in range(HG_HEADS):
        lanes = slice(h * HG_HEAD_DIM, (h + 1) * HG_HEAD_DIM)
        o = of_ref[0, :, lanes].astype(F32) + ob_ref[0, :, lanes].astype(F32)
        o = o * lax.rsqrt(jnp.mean(o * o, axis=-1, keepdims=True) + EPS)
        y = o * g_ref[:, lanes] * _silu(gate_ref[0, :, lanes])
        o_ref[0, :, lanes] = y.astype(o_ref.dtype)


def _hg_post(o_f, o_b, p3, norm_g, tt=512):
    bsz, seq_len, _ = p3.shape
    blk = pl.BlockSpec((1, tt, W_BRANCH), lambda b, t: (b, t, 0))
    return pl.pallas_call(
        _hg_post_kernel,
        grid=(bsz, seq_len // tt),
        in_specs=[blk, blk,
                  pl.BlockSpec((1, tt, W_BRANCH), lambda b, t: (b, t, COL_C_GATE)),
                  pl.BlockSpec((1, W_BRANCH), lambda b, t: (0, 0))],
        out_specs=blk,
        out_shape=jax.ShapeDtypeStruct((bsz, seq_len, W_BRANCH), BF16),
        compiler_params=_cparams(("parallel", "parallel")),
        name="hgrn2_post",
    )(o_f, o_b, p3, norm_g)


def _s5_prep_kernel(are_ref, aim_ref, ldt_ref, bre_ref, bim_ref, bbr_ref, bbi_ref, pwr_ref, pwi_ref):
    a_re = are_ref[...]
    a_im = aim_ref[...]
    dt = jnp.exp(ldt_ref[...])
    mag = jnp.exp(dt * a_re)
    ang = dt * a_im
    ab_re = mag * jnp.cos(ang)
    ab_im = mag * jnp.sin(ang)
    den = a_re * a_re + a_im * a_im
    x_ = ab_re - 1.0
    y_ = ab_im
    g_re = (x_ * a_re + y_ * a_im) / den
    g_im = (y_ * a_re - x_ * a_im) / den
    for c in range(S5_GROUP_CH):
        b_re = bre_ref[c]
        b_im = bim_ref[c]
        bbr_ref[c] = g_re * b_re - g_im * b_im
        bbi_ref[c] = g_re * b_im + g_im * b_re
    p_re = ab_re
    p_im = ab_im
    for i in range(S5_SUB):
        pwr_ref[i] = p_re
        pwi_ref[i] = p_im
        p_re, p_im = p_re * ab_re - p_im * ab_im, p_re * ab_im + p_im * ab_re


def _s5_prep(a_re, a_im, log_dt, b_re, b_im):
    n = 2 * S5_GROUPS
    sb = jax.ShapeDtypeStruct((S5_GROUP_CH, n, S5_STATE), F32)
    spw = jax.ShapeDtypeStruct((S5_SUB, n, S5_STATE), F32)
    return pl.pallas_call(
        _s5_prep_kernel,
        out_shape=[sb, sb, spw, spw],
        name="s5_prep",
    )(a_re.reshape(n, S5_STATE), a_im.reshape(n, S5_STATE), log_dt.reshape(n, 1),
      jnp.transpose(b_re.reshape(n, S5_STATE, S5_GROUP_CH), (2, 0, 1)),
      jnp.transpose(b_im.reshape(n, S5_STATE, S5_GROUP_CH), (2, 0, 1)))


def _s5_layout(bbr, bbi, pwr, pwi, c_re, c_im):
    eye = jnp.eye(S5_GPT, dtype=F32)

    def pw_tiles(x):
        return jnp.transpose(x.reshape(S5_SUB, 2, S5_LT, S5_SW), (1, 2, 0, 3))

    pw_t = jnp.stack([pw_tiles(pwr), pw_tiles(pwi)], axis=1)

    def b_bd(x):
        x = jnp.transpose(x, (1, 0, 2)).reshape(2, S5_LT, S5_GPT, S5_GROUP_CH, S5_STATE)
        bd = x[:, :, :, :, None, :] * eye[None, None, :, None, :, None]
        return bd.reshape(2, S5_LT, LANES, S5_SW)

    b_t = jnp.concatenate([b_bd(bbr), b_bd(bbi)], axis=-1)

    def c_bd(x):
        x = jnp.transpose(x.reshape(2, S5_LT, S5_GPT, S5_GROUP_CH, S5_STATE), (0, 1, 2, 4, 3))
        bd = x[:, :, :, :, None, :] * eye[None, None, :, None, :, None]
        return bd.reshape(2, S5_LT, S5_SW, LANES)

    pw_c = jnp.transpose(pw_t, (0, 1, 2, 4, 3))
    return pw_t, pw_c, b_t, c_bd(c_re), c_bd(-c_im)


def _s5_operators(prep, c_re, c_im):
    pw_t, pw_c, b_t, cre_t, cimn_t = _s5_layout(*prep, c_re, c_im)
    b8_t, kint_t, ca_t = _s5_block_prep(pw_t, pw_c, b_t, cre_t, cimn_t)
    return pw_t, b8_t, kint_t, ca_t


def _s5_perm():
    n = SUBLANES * SUBLANES
    r = np.arange(n)
    p = np.zeros((n, n), np.float32)
    p[r, (r % SUBLANES) * SUBLANES + r // SUBLANES] = 1.0
    return p


def _dot3(x, y):
    xh = x.astype(BF16)
    xl = (x - xh.astype(F32)).astype(BF16)
    yh = y.astype(BF16)
    yl = (y - yh.astype(F32)).astype(BF16)
    return (jnp.dot(xh, yh, preferred_element_type=F32) + jnp.dot(xh, yl, preferred_element_type=F32)
            + jnp.dot(xl, yh, preferred_element_type=F32))


def _s5_block_kernel(b_ref, cre_ref, cimn_ref, pw_ref, pwc_ref, b8_ref, kint_ref, ca_ref):
    n = S5_BLK
    for d in range(2):
        b = b_ref[d, 0]
        cre = cre_ref[d, 0]
        cimn = cimn_ref[d, 0]
        cfull = jnp.concatenate([cre, cimn], axis=0)

        def b_pow(k, d=d, b=b):
            if k == 0:
                return b
            pr = pw_ref[d, 0, 0, k - 1:k, :]
            pi = pw_ref[d, 1, 0, k - 1:k, :]
            bre, bim = b[:, :S5_SW], b[:, S5_SW:]
            return jnp.concatenate([bre * pr - bim * pi, bre * pi + bim * pr], axis=1)

        def c_pow(k, d=d, cre=cre, cimn=cimn):
            ar = pwc_ref[d, 0, 0, :, k - 1:k]
            ai = pwc_ref[d, 1, 0, :, k - 1:k]
            return jnp.concatenate([cre * ar + cimn * ai, cimn * ar - cre * ai], axis=0)

        bp = [b_pow(k) for k in range(n)]
        kk = [_dot3(bp[k], cfull).astype(BF16) for k in range(n)]
        zero = jnp.zeros((LANES, LANES), BF16)
        for s in range(n):
            after = n - 1 - s if d == 0 else s
            b8_ref[d, 0, s * LANES:(s + 1) * LANES, :] = bp[after].astype(BF16)
            upto = s + 1 if d == 0 else n - s
            ca_ref[d, 0, :, s * LANES:(s + 1) * LANES] = c_pow(upto).astype(BF16)
            for r in range(n):
                lag = r - s if d == 0 else s - r
                kint_ref[d, 0, s * LANES:(s + 1) * LANES, r * LANES:(r + 1) * LANES] = kk[lag] if lag >= 0 else zero


def _s5_block_prep(pw_t, pw_c, b_t, cre_t, cimn_t):
    w = S5_BLK * LANES
    out = jax.ShapeDtypeStruct((2, S5_LT, w, w), BF16)
    out_b8 = jax.ShapeDtypeStruct((2, S5_LT, w, 2 * S5_SW), BF16)
    out_ca = jax.ShapeDtypeStruct((2, S5_LT, 2 * S5_SW, w), BF16)
    return pl.pallas_call(
        _s5_block_kernel,
        grid=(S5_LT,),
        in_specs=[pl.BlockSpec((2, 1, LANES, 2 * S5_SW), lambda l: (0, l, 0, 0)),
                  pl.BlockSpec((2, 1, S5_SW, LANES), lambda l: (0, l, 0, 0)),
                  pl.BlockSpec((2, 1, S5_SW, LANES), lambda l: (0, l, 0, 0)),
                  pl.BlockSpec((2, 2, 1, S5_SUB, S5_SW), lambda l: (0, 0, l, 0, 0)),
                  pl.BlockSpec((2, 2, 1, S5_SW, S5_SUB), lambda l: (0, 0, l, 0, 0))],
        out_specs=[pl.BlockSpec((2, 1, w, 2 * S5_SW), lambda l: (0, l, 0, 0)),
                   pl.BlockSpec((2, 1, w, w), lambda l: (0, l, 0, 0)),
                   pl.BlockSpec((2, 1, 2 * S5_SW, w), lambda l: (0, l, 0, 0))],
        out_shape=[out_b8, out, out_ca],
        compiler_params=_cparams(("parallel",)),
        name="s5_block_prep",
    )(b_t, cre_t, cimn_t, pw_t, pw_c)


def _vreg_grid_t(x):
    g = SUBLANES
    return jnp.concatenate(
        [jnp.concatenate([x[q * g:(q + 1) * g, p * LANES:(p + 1) * LANES] for q in range(g)], axis=1)
         for p in range(g)], axis=0)


def _s5_to_wide(x):
    g = SUBLANES
    return jnp.concatenate(
        [jnp.concatenate([x[(p * g + q) * g:(p * g + q + 1) * g, :] for q in range(g)], axis=1) for p in range(g)],
        axis=0)


def _s5_from_wide(x):
    g = SUBLANES
    return jnp.concatenate([x[p * g:(p + 1) * g, q * LANES:(q + 1) * LANES] for p in range(g) for q in range(g)],
                           axis=0)


def _s5_kernel(u_ref, pw_ref, b8_ref, kint_ref, ca_ref, d_ref, pm_ref, o_ref,
               drv_scr, y8_scr, hin_scr, *, seq_len):
    n_g = seq_len // (S5_TILE * S5_GROUP)
    nb = S5_SUB // S5_BLK
    rpt = SUBLANES * nb
    w = SUBLANES * LANES
    o_ref[0] = u_ref[0] * d_ref[...]
    zero_row = jnp.zeros((1, S5_SW), F32)
    zero_blk = jnp.zeros((SUBLANES, S5_SW), F32)

    def power(d, k):
        return pw_ref[d, 0, 0, k - 1:k, :], pw_ref[d, 1, 0, k - 1:k, :]

    def tile_rows(kg, i, d):
        t = kg * S5_GROUP + i
        kt = t if d == 0 else n_g * S5_GROUP - 1 - t
        return pl.ds(pl.multiple_of(kt * S5_TILE, S5_TILE), S5_TILE)

    def scan_tile(d, i, carry_re, carry_im):
        order = list(range(nb)) if d == 0 else list(range(nb - 1, -1, -1))
        ar, ai = (jnp.broadcast_to(x, (SUBLANES, S5_SW)) for x in power(d, S5_BLK))
        hr, hi = zero_blk, zero_blk
        loc = {}
        for m in order:
            rows = slice(i * rpt + m * SUBLANES, i * rpt + (m + 1) * SUBLANES)
            hr, hi = (ar * hr - ai * hi + drv_scr[d, rows, :S5_SW], ar * hi + ai * hr + drv_scr[d, rows, S5_SW:])
            loc[m] = (hr, hi)
        asr, asi = power(d, S5_SUB)
        c_re, c_im = carry_re, carry_im
        ent_re, ent_im = [None] * SUBLANES, [None] * SUBLANES
        for j in (range(SUBLANES) if d == 0 else range(SUBLANES - 1, -1, -1)):
            ent_re[j], ent_im[j] = c_re, c_im
            er, ei = hr[j:j + 1, :], hi[j:j + 1, :]
            c_re, c_im = er + asr * c_re - asi * c_im, ei + asr * c_im + asi * c_re
        ent_re = jnp.concatenate(ent_re, axis=0)
        ent_im = jnp.concatenate(ent_im, axis=0)
        prev_re, prev_im = {order[0]: ent_re}, {order[0]: ent_im}
        for idx, m in enumerate(order[:-1]):
            pr, pi = (jnp.broadcast_to(x, (SUBLANES, S5_SW)) for x in power(d, S5_BLK * (idx + 1)))
            prev_re[order[idx + 1]] = loc[m][0] + pr * ent_re - pi * ent_im
            prev_im[order[idx + 1]] = loc[m][1] + pr * ent_im + pi * ent_re
        h_in = jnp.concatenate([jnp.concatenate([prev_re[m] for m in range(nb)], axis=0),
                                jnp.concatenate([prev_im[m] for m in range(nb)], axis=0)], axis=1)
        hin_scr[d, i * rpt:(i + 1) * rpt, :] = h_in.astype(BF16)
        return c_re, c_im

    def group_body(kg, carry):
        for d in range(2):
            wide = jnp.concatenate([_s5_to_wide(u_ref[0, tile_rows(kg, i, d), :]) for i in range(S5_GROUP)], axis=1)
            perm = jnp.dot(pm_ref[...], wide.astype(BF16), preferred_element_type=F32)
            u8 = jnp.concatenate([_vreg_grid_t(perm[:, i * w:(i + 1) * w]) for i in range(S5_GROUP)],
                                 axis=0).astype(BF16)
            drv_scr[d] = jnp.dot(u8, b8_ref[d, 0], preferred_element_type=F32)
            y8_scr[d] = jnp.dot(u8, kint_ref[d, 0], preferred_element_type=F32)
        new_carry = []
        for d in range(2):
            c_re, c_im = carry[2 * d], carry[2 * d + 1]
            for i in range(S5_GROUP):
                c_re, c_im = scan_tile(d, i, c_re, c_im)
            new_carry += [c_re, c_im]
        for d in range(2):
            y = y8_scr[d] + jnp.dot(hin_scr[d], ca_ref[d, 0], preferred_element_type=F32)
            yt = jnp.concatenate([_vreg_grid_t(y[i * rpt:(i + 1) * rpt]) for i in range(S5_GROUP)], axis=1)
            y_hi = yt.astype(BF16)
            y_lo = (yt - y_hi.astype(F32)).astype(BF16)
            yn = (jnp.dot(pm_ref[...], y_hi, preferred_element_type=F32)
                  + jnp.dot(pm_ref[...], y_lo, preferred_element_type=F32))
            for i in range(S5_GROUP):
                rows = tile_rows(kg, i, d)
                o_ref[0, rows, :] = o_ref[0, rows, :] + _s5_from_wide(yn[:, i * w:(i + 1) * w])
        return tuple(new_carry)

    lax.fori_loop(0, n_g, group_body, (zero_row,) * 4)


def _s5_scan(p3, pw_t, b8_t, kint_t, ca_t, d_skip, perm):
    bsz, seq_len, _ = p3.shape
    cpb = W_BRANCH // LANES
    w = S5_BLK * LANES
    rows = S5_GROUP * SUBLANES * (S5_SUB // S5_BLK)
    return pl.pallas_call(
        functools.partial(_s5_kernel, seq_len=seq_len),
        grid=(S5_LT, bsz),
        in_specs=[pl.BlockSpec((1, seq_len, LANES), lambda l, b: (b, 0, COL_D_IN * cpb + l)),
                  pl.BlockSpec((2, 2, 1, S5_SUB, S5_SW), lambda l, b: (0, 0, l, 0, 0)),
                  pl.BlockSpec((2, 1, w, 2 * S5_SW), lambda l, b: (0, l, 0, 0)),
                  pl.BlockSpec((2, 1, w, w), lambda l, b: (0, l, 0, 0)),
                  pl.BlockSpec((2, 1, 2 * S5_SW, w), lambda l, b: (0, l, 0, 0)),
                  pl.BlockSpec((1, LANES), lambda l, b: (0, l)),
                  pl.BlockSpec((SUBLANES * SUBLANES, SUBLANES * SUBLANES), lambda l, b: (0, 0))],
        out_specs=pl.BlockSpec((1, seq_len, LANES), lambda l, b: (b, 0, l)),
        out_shape=jax.ShapeDtypeStruct((bsz, seq_len, W_BRANCH), F32),
        scratch_shapes=[pltpu.VMEM((2, rows, 2 * S5_SW), F32),
                        pltpu.VMEM((2, rows, w), F32),
                        pltpu.VMEM((2, rows, 2 * S5_SW), BF16)],
        compiler_params=_cparams(("parallel", "parallel")),
        name="s5_scan",
    )(p3, pw_t, b8_t, kint_t, ca_t, d_skip, perm)


def _s5_post_kernel(y_ref, gate_ref, w_ref, b_ref, o_ref):
    y = y_ref[0]
    c0 = math.sqrt(2.0 / math.pi)
    z = 0.5 * y * (1.0 + jnp.tanh(c0 * (y + 0.044715 * (y * y * y))))
    lin = jnp.dot(z.astype(BF16), w_ref[...], preferred_element_type=F32) + b_ref[...]
    out = z * _sigmoid(lin) * _silu(gate_ref[0])
    o_ref[0] = out.astype(o_ref.dtype)


def _s5_post(y, p3, glu_w_bf, glu_b, tt=512):
    bsz, seq_len, _ = p3.shape
    blk = pl.BlockSpec((1, tt, W_BRANCH), lambda b, t: (b, t, 0))
    return pl.pallas_call(
        _s5_post_kernel,
        grid=(bsz, seq_len // tt),
        in_specs=[blk,
                  pl.BlockSpec((1, tt, W_BRANCH), lambda b, t: (b, t, COL_D_GATE)),
                  pl.BlockSpec((W_BRANCH, W_BRANCH), lambda b, t: (0, 0)),
                  pl.BlockSpec((1, W_BRANCH), lambda b, t: (0, 0))],
        out_specs=blk,
        out_shape=jax.ShapeDtypeStruct((bsz, seq_len, W_BRANCH), BF16),
        compiler_params=_cparams(("parallel", "parallel")),
        name="s5_post",
    )(y, p3, glu_w_bf, glu_b)


def _merge_kernel(ya_ref, yb_ref, yc_ref, yd_ref, r0_ref, r1_ref, r2_ref, r3_ref, w_ref, o_ref):
    acc = None
    for i, (y_ref, r_ref) in enumerate(((ya_ref, r0_ref), (yb_ref, r1_ref), (yc_ref, r2_ref), (yd_ref, r3_ref))):
        term = _sigmoid(r_ref[...].astype(F32)) * jnp.dot(y_ref[...], w_ref[i], preferred_element_type=F32)
        acc = term if acc is None else acc + term
    o_ref[...] = acc.astype(o_ref.dtype)


def _merge(ys, r2, w_stack_bf, tm=512, tn=1024):
    t = r2.shape[0]
    npb = D_MODEL // tn
    yspec = pl.BlockSpec((tm, W_BRANCH), lambda j, i: (i, 0))

    def rspec(br):
        return pl.BlockSpec((tm, tn), lambda j, i: (i, br * npb + j))

    return pl.pallas_call(
        _merge_kernel,
        grid=(npb, t // tm),
        in_specs=[yspec, yspec, yspec, yspec, rspec(0), rspec(1), rspec(2), rspec(3),
                  pl.BlockSpec((N_BRANCH, W_BRANCH, tn), lambda j, i: (0, 0, j))],
        out_specs=pl.BlockSpec((tm, tn), lambda j, i: (i, j)),
        out_shape=jax.ShapeDtypeStruct((t, D_MODEL), BF16),
        compiler_params=_cparams(("parallel", "parallel")),
        name="merge",
    )(*ys, r2, r2, r2, r2, w_stack_bf)


def _out_proj_kernel(x_ref, m_ref, w_ref, g_ref, o_ref, *, final_norm):
    y = x_ref[...] + jnp.dot(m_ref[...], w_ref[...], preferred_element_type=F32)
    if final_norm:
        y = y * lax.rsqrt(jnp.mean(y * y, axis=-1, keepdims=True) + EPS) * g_ref[...]
    o_ref[...] = y


def _out_proj(x2, m, w_bf, final_g, final_norm, tm=512):
    t = x2.shape[0]
    blk = pl.BlockSpec((tm, D_MODEL), lambda i: (i, 0))
    return pl.pallas_call(
        functools.partial(_out_proj_kernel, final_norm=final_norm),
        grid=(t // tm,),
        in_specs=[blk, blk,
                  pl.BlockSpec((D_MODEL, D_MODEL), lambda i: (0, 0)),
                  pl.BlockSpec((1, D_MODEL), lambda i: (0, 0))],
        out_specs=blk,
        out_shape=jax.ShapeDtypeStruct((t, D_MODEL), F32),
        compiler_params=_cparams(("parallel",)),
        name="out_proj",
    )(x2, m, w_bf, final_g)


def _lb_kernel(x_ref, o_ref):
    x = x_ref[...]
    e = jnp.exp(x - jnp.max(x, axis=0, keepdims=True))
    sm = e / jnp.sum(e, axis=0, keepdims=True)
    run = jnp.zeros_like(sm[0])
    for l in range(DEPTH):
        run = run + sm[l]
        o_ref[l] = run - sm[0]


def _lower_bounds(hg_lb):
    return pl.pallas_call(
        _lb_kernel,
        out_shape=jax.ShapeDtypeStruct(hg_lb.shape, F32),
        name="hg_lower_bounds",
    )(hg_lb)


def _layer(x, lw, final_g, final_norm):
    bsz, seq_len, _ = x.shape
    x2 = x.reshape(bsz * seq_len, D_MODEL)
    p2, r2 = _in_proj(x2, lw["norm_g"], lw["w_in"])
    p3 = p2.reshape(bsz, seq_len, COL_R * W_BRANCH)
    ya = _conv_branch(p3, lw["conv_w"], lw["conv_b"], lw["conv_ln_g"], lw["conv_ln_b"])
    yb = _pool_branch(p3, lw["pool_w"], lw["pool_scale"])
    o_f, o_b = _hg_scan(p3, lw["lb"], lw["hg_nm"], lw["hg_mk"])
    yc = _hg_post(o_f, o_b, p3, lw["hg_norm_g"])
    y5 = _s5_scan(p3, *lw["s5"], lw["s5_d"], lw["s5_perm"])
    yd = _s5_post(y5, p3, lw["s5_glu_w"], lw["s5_glu_b"])
    ys = [y.reshape(bsz * seq_len, W_BRANCH) for y in (ya, yb, yc, yd)]
    m = _merge(ys, r2, lw["w_branch_out"])
    out = _out_proj(x2, m, lw["w_out"], final_g, final_norm)
    return out.reshape(bsz, seq_len, D_MODEL)


def kernel(x_prompt, x_sample, norm_g, w_in, conv_w, conv_b, conv_ln_g, conv_ln_b, w_a_out, pool_w, pool_scale, w_b_out, hg_lb, hg_norm_g, w_c_out, s5_a_re, s5_a_im, s5_log_dt, s5_b_re, s5_b_im, s5_c_re, s5_c_im, s5_d, s5_glu_w, s5_glu_b, w_d_out, w_out, final_g):
    lb_all = _lower_bounds(hg_lb)
    nm, mk = _hg_tables()
    nm_bf = jnp.asarray(nm, BF16)
    mk_bf = jnp.asarray(mk, BF16)
    perm_bf = jnp.asarray(_s5_perm(), BF16)
    layers = []
    for l in range(DEPTH):
        prep = _s5_prep(s5_a_re[l], s5_a_im[l], s5_log_dt[l], s5_b_re[l], s5_b_im[l])
        layers.append(dict(
            norm_g=norm_g[l][None], w_in=w_in[l].astype(BF16),
            conv_w=conv_w[l], conv_b=conv_b[l][None], conv_ln_g=conv_ln_g[l][None], conv_ln_b=conv_ln_b[l][None],
            pool_w=pool_w[l].astype(BF16), pool_scale=pool_scale[l][None],
            lb=lb_all[l], hg_nm=nm_bf, hg_mk=mk_bf, hg_norm_g=hg_norm_g[l][None],
            s5=_s5_operators(prep, s5_c_re[l], s5_c_im[l]), s5_d=s5_d[l][None],
            s5_perm=perm_bf,
            s5_glu_w=s5_glu_w[l].astype(BF16), s5_glu_b=s5_glu_b[l][None],
            w_branch_out=jnp.stack([w_a_out[l], w_b_out[l], w_c_out[l], w_d_out[l]]).astype(BF16),
            w_out=w_out[l].astype(BF16)))
    fg = final_g[None]

    def run(x):
        for l in range(DEPTH):
            x = _layer(x, layers[l], fg, l == DEPTH - 1)
        return x

    return (run(x_prompt), run(x_sample))
```

```python
import functools
import math

import numpy as np
import jax
import jax.numpy as jnp
from jax import lax
from jax.experimental import pallas as pl
from jax.experimental.pallas import tpu as pltpu

F32 = jnp.float32
BF16 = jnp.bfloat16

D_MODEL = 2048
DEPTH = 2
W_BRANCH = 1024
N_BRANCH = 4
N_IN = 12 * W_BRANCH + N_BRANCH * D_MODEL
CONV_WIDTH = 31
CONV_PAD = CONV_WIDTH // 2
POOL_WINDOWS = (2, 4, 8, 16)
POOL_GROUP = W_BRANCH // len(POOL_WINDOWS)
HG_HEAD_DIM = 128
HG_HEADS = W_BRANCH // HG_HEAD_DIM
S5_GROUP_CH = 16
S5_GROUPS = W_BRANCH // S5_GROUP_CH
S5_STATE = 64
EPS = 1e-6

LANES = 128
SUBLANES = 8
VMEM_LIMIT = 52 * 1024 * 1024

COL_A_VAL, COL_A_GLU, COL_A_GATE, COL_B_IN, COL_B_GATE = 0, 1, 2, 3, 4
COL_C_Q, COL_C_FF, COL_C_FB, COL_C_I, COL_C_GATE, COL_D_IN, COL_D_GATE = 5, 6, 7, 8, 9, 10, 11
COL_R = 12

HALO = 16
CONV_LANES = 256
HG_TILE = 256
HG_HALF = HG_TILE // 2
HG_HPS = 8
HG_LEVELS = 8
HG_NROWS = 4 * HG_TILE
S5_TILE = 512
S5_SUB = S5_TILE // SUBLANES
S5_BLK = 8
S5_GROUP = 4
S5_LT = W_BRANCH // LANES
S5_GPT = LANES // S5_GROUP_CH
S5_SW = S5_GPT * S5_STATE


def _sigmoid(x):
    return 0.5 * jnp.tanh(0.5 * x) + 0.5


def _silu(x):
    return x * _sigmoid(x)


def _cparams(sem):
    return pltpu.CompilerParams(dimension_semantics=sem, vmem_limit_bytes=VMEM_LIMIT)


def _in_proj_kernel(x_ref, g_ref, w_ref, o_ref, r_ref, h_ref, *, n_main):
    j = pl.program_id(1)

    @pl.when(j == 0)
    def _():
        x = x_ref[...]
        ms = jnp.mean(x * x, axis=-1, keepdims=True)
        h_ref[...] = (x * lax.rsqrt(ms + EPS) * g_ref[...]).astype(BF16)

    @pl.when(j < n_main)
    def _():
        o_ref[...] = jnp.dot(h_ref[...], w_ref[...], preferred_element_type=F32)

    @pl.when(j >= n_main)
    def _():
        r_ref[...] = jnp.dot(h_ref[...], w_ref[...], preferred_element_type=F32).astype(BF16)


def _in_proj(x2, g, w_bf, tm=1024, tn=1024):
    t = x2.shape[0]
    n_main = COL_R * W_BRANCH // tn
    return pl.pallas_call(
        functools.partial(_in_proj_kernel, n_main=n_main),
        grid=(t // tm, N_IN // tn),
        in_specs=[pl.BlockSpec((tm, D_MODEL), lambda i, j: (i, 0)),
                  pl.BlockSpec((1, D_MODEL), lambda i, j: (0, 0)),
                  pl.BlockSpec((D_MODEL, tn), lambda i, j: (0, j))],
        out_specs=[pl.BlockSpec((tm, tn), lambda i, j: (i, jnp.minimum(j, n_main - 1))),
                   pl.BlockSpec((tm, tn), lambda i, j: (i, jnp.maximum(j - n_main, 0)))],
        out_shape=[jax.ShapeDtypeStruct((t, COL_R * W_BRANCH), F32),
                   jax.ShapeDtypeStruct((t, N_BRANCH * D_MODEL), BF16)],
        scratch_shapes=[pltpu.VMEM((tm, D_MODEL), BF16)],
        compiler_params=_cparams(("parallel", "arbitrary")),
        name="in_proj",
    )(x2, g, w_bf)


def _conv_kernel(main_ref, gate_ref, prev_ref, next_ref, cw_ref, cb_ref, lg_ref, lb_ref, o_ref,
                 us_scr, wb_scr, *, tt, rc):
    t = pl.program_id(1)
    n_t = pl.num_programs(1)

    def glu(ref):
        return ref[0, :, :W_BRANCH] * _sigmoid(ref[0, :, W_BRANCH:])

    us_scr[0, HALO:HALO + tt, :] = glu(main_ref)
    us_scr[0, 0:HALO, :] = jnp.where(t > 0, glu(prev_ref), 0.0)
    us_scr[0, HALO + tt:2 * HALO + tt, :] = jnp.where(t < n_t - 1, glu(next_ref), 0.0)
    n = tt + 2 * HALO - SUBLANES
    for s in range(1, SUBLANES):
        us_scr[s, 0:n, :] = us_scr[0, s:s + n, :]
    for j in range(CONV_WIDTH):
        wb_scr[j] = jnp.broadcast_to(cw_ref[j:j + 1, :], (SUBLANES, W_BRANCH))

    def body(c, carry):
        r0 = pl.multiple_of(c * rc, rc)
        parts = []
        for lc in range(W_BRANCH // CONV_LANES):
            lanes = slice(lc * CONV_LANES, (lc + 1) * CONV_LANES)
            acc = jnp.zeros((rc // SUBLANES, SUBLANES, CONV_LANES), F32)
            for j in range(CONV_WIDTH):
                off = HALO - CONV_PAD + j
                s = off % SUBLANES
                tap = us_scr[s, pl.ds(r0 + (off - s), rc), lanes]
                acc = acc + tap.reshape(rc // SUBLANES, SUBLANES, CONV_LANES) * wb_scr[j, :, lanes]
            parts.append(acc.reshape(rc, CONV_LANES))
        y = jnp.concatenate(parts, axis=-1) + cb_ref[...]
        mu = jnp.mean(y, axis=-1, keepdims=True)
        yc = y - mu
        var = jnp.mean(yc * yc, axis=-1, keepdims=True)
        yn = yc * lax.rsqrt(var + EPS) * lg_ref[...] + lb_ref[...]
        out = _silu(yn) * _silu(gate_ref[0, pl.ds(r0, rc), :])
        o_ref[0, pl.ds(r0, rc), :] = out.astype(o_ref.dtype)
        return carry

    lax.fori_loop(0, tt // rc, body, 0)


def _halo_specs(tt, width, col_blk, seq_len):
    per = tt // HALO
    last = seq_len // HALO - 1
    prev = pl.BlockSpec((1, HALO, width), lambda b, t: (b, jnp.maximum(t * per - 1, 0), col_blk))
    nxt = pl.BlockSpec((1, HALO, width), lambda b, t: (b, jnp.minimum((t + 1) * per, last), col_blk))
    return prev, nxt


def _conv_branch(p3, cw, cb, lg, lb, tt=512, rc=32):
    bsz, seq_len, _ = p3.shape
    prev, nxt = _halo_specs(tt, 2 * W_BRANCH, 0, seq_len)
    vec = pl.BlockSpec((1, W_BRANCH), lambda b, t: (0, 0))
    return pl.pallas_call(
        functools.partial(_conv_kernel, tt=tt, rc=rc),
        grid=(bsz, seq_len // tt),
        in_specs=[pl.BlockSpec((1, tt, 2 * W_BRANCH), lambda b, t: (b, t, 0)),
                  pl.BlockSpec((1, tt, W_BRANCH), lambda b, t: (b, t, COL_A_GATE)),
                  prev, nxt,
                  pl.BlockSpec((CONV_WIDTH, W_BRANCH), lambda b, t: (0, 0)),
                  vec, vec, vec],
        out_specs=pl.BlockSpec((1, tt, W_BRANCH), lambda b, t: (b, t, 0)),
        out_shape=jax.ShapeDtypeStruct((bsz, seq_len, W_BRANCH), BF16),
        scratch_shapes=[pltpu.VMEM((SUBLANES, tt + 2 * HALO, W_BRANCH), F32),
                        pltpu.VMEM((CONV_WIDTH, SUBLANES, W_BRANCH), F32)],
        compiler_params=_cparams(("parallel", "parallel")),
        name="conv_branch",
    )(p3, p3, p3, p3, cw, cb, lg, lb)


def _pool_kernel(main_ref, gate_ref, prev_ref, next_ref, pw_ref, ps_ref, o_ref, e_scr, s_scr,
                 *, tt, seq_len):
    t = pl.program_id(1)
    n_t = pl.num_programs(1)
    rows = tt + 2 * HALO
    e_scr[HALO:HALO + tt, :] = main_ref[0]
    e_scr[0:HALO, :] = jnp.where(t > 0, prev_ref[0], 0.0)
    e_scr[HALO + tt:rows, :] = jnp.where(t < n_t - 1, next_ref[0], 0.0)

    tpos = t * tt + lax.broadcasted_iota(jnp.int32, (tt, 1), 0)
    for g, win in enumerate(POOL_WINDOWS):
        lanes = slice(g * POOL_GROUP, (g + 1) * POOL_GROUP)
        n = rows - 2
        s_scr[1:1 + n, lanes] = e_scr[1:1 + n, lanes] + e_scr[0:n, lanes]
        half = 1
        while 2 * half < win:
            n = rows - 2 * half - 2 * half
            lo = 2 * half
            a = s_scr[lo + half:lo + half + n, lanes]
            b = s_scr[lo - half:lo - half + n, lanes]
            s_scr[lo:lo + n, lanes] = a + b
            half *= 2
        left = win // 2
        right = win - 1 - left
        cnt = (jnp.minimum(tpos + right, seq_len - 1) + 1 - jnp.maximum(tpos - left, 0)).astype(F32)
        u = e_scr[HALO:HALO + tt, lanes]
        d = s_scr[HALO:HALO + tt, lanes] / cnt - u
        y = jnp.dot(d.astype(BF16), pw_ref[g], preferred_element_type=F32)
        y = y * ps_ref[:, lanes] * _silu(gate_ref[0, :, lanes])
        o_ref[0, :, lanes] = y.astype(o_ref.dtype)


def _pool_branch(p3, pw_bf, ps, tt=512):
    bsz, seq_len, _ = p3.shape
    prev, nxt = _halo_specs(tt, W_BRANCH, COL_B_IN, seq_len)
    return pl.pallas_call(
        functools.partial(_pool_kernel, tt=tt, seq_len=seq_len),
        grid=(bsz, seq_len // tt),
        in_specs=[pl.BlockSpec((1, tt, W_BRANCH), lambda b, t: (b, t, COL_B_IN)),
                  pl.BlockSpec((1, tt, W_BRANCH), lambda b, t: (b, t, COL_B_GATE)),
                  prev, nxt,
                  pl.BlockSpec((len(POOL_WINDOWS), POOL_GROUP, POOL_GROUP), lambda b, t: (0, 0, 0)),
                  pl.BlockSpec((1, W_BRANCH), lambda b, t: (0, 0))],
        out_specs=pl.BlockSpec((1, tt, W_BRANCH), lambda b, t: (b, t, 0)),
        out_shape=jax.ShapeDtypeStruct((bsz, seq_len, W_BRANCH), BF16),
        scratch_shapes=[pltpu.VMEM((tt + 2 * HALO, W_BRANCH), F32),
                        pltpu.VMEM((tt + 2 * HALO, W_BRANCH), F32)],
        compiler_params=_cparams(("parallel", "parallel")),
        name="pool_branch",
    )(p3, p3, p3, p3, pw_bf, ps)


def _hg_tables():
    c = HG_TILE
    t = np.arange(c)
    rr = np.arange(c)[None, :]
    vstart = (t // SUBLANES) * SUBLANES
    pref = [t, vstart + 7, vstart + 3, vstart + np.where(t % SUBLANES < 4, 1, 5)]
    nmat = np.stack([rr <= p[:, None] for p in pref]).astype(np.float32)
    masks = np.zeros((HG_LEVELS, HG_HALF, HG_HALF), np.float32)
    masks[0] = np.eye(HG_HALF)
    th = np.arange(HG_HALF)
    for lvl in range(HG_LEVELS - 1):
        half = 1 << lvl
        bh = th // (2 * half)
        lh = (th % (2 * half)) >= half
        masks[1 + lvl] = (bh[:, None] == bh[None, :]) & lh[:, None] & (~lh)[None, :]
    nm = np.stack([nmat, nmat[:, ::-1, ::-1]]).reshape(2, HG_NROWS, c)
    mk = np.stack([masks, masks[:, ::-1, ::-1]])
    return nm, mk


def _dot_nt(a, b):
    return lax.dot_general(a, b, (((1,), (1,)), ((), ())), preferred_element_type=F32)


def _hg_ref_rows(lvl, d):
    out = []
    for m in range(HG_TILE // SUBLANES):
        block = 2 << lvl
        mid = (m * SUBLANES // block) * block + block // 2
        out.append(mid // SUBLANES - 1 if d == 0 else mid // SUBLANES)
    return out


def _hg_gates(q, z, v, lbv, nm_ref, d):
    c = HG_TILE
    sg = _sigmoid(z)
    f = lbv + (1.0 - lbv) * sg
    lf = jnp.log2(f)
    kk = (1.0 - lbv) * (1.0 - sg)
    hi = lf.astype(BF16)
    lo = (lf - hi.astype(F32)).astype(BF16)
    tab = jnp.dot(nm_ref[d], jnp.concatenate([hi, lo], axis=1), preferred_element_type=F32)
    tab = tab[:, :HG_HEAD_DIM] + tab[:, HG_HEAD_DIM:]
    b, b_grp, b_l2, b_l1 = (tab[i * c:(i + 1) * c] for i in range(4))
    b_end = b_grp[c - 1:c] if d == 0 else b_grp[0:1]
    n_g = c // SUBLANES
    sub = lax.broadcasted_iota(jnp.int32, (1, SUBLANES, HG_HEAD_DIM), 1)
    q3 = q.reshape(n_g, SUBLANES, HG_HEAD_DIM)
    k3 = kk.reshape(n_g, SUBLANES, HG_HEAD_DIM)
    xs = []
    for lvl in range(HG_LEVELS):
        if (1 << lvl) < SUBLANES:
            later = (jnp.right_shift(sub, lvl) & 1) == (1 - d)
            base = jnp.where(later, q3, k3).reshape(c, HG_HEAD_DIM)
        else:
            pick = [((m * SUBLANES) >> lvl) & 1 == (1 - d) for m in range(n_g)]
            base = jnp.concatenate([(q3 if p else k3)[m] for m, p in enumerate(pick)], axis=0)
        if lvl == 0:
            later0 = jnp.broadcast_to(later, (n_g, SUBLANES, HG_HEAD_DIM)).reshape(c, HG_HEAD_DIM)
            xs.append(jnp.where(later0, base * f, base))
            continue
        if lvl == 1:
            ref = b_l1
        elif lvl == 2:
            ref = b_l2
        else:
            ref = jnp.concatenate([b_grp[g * SUBLANES:(g + 1) * SUBLANES] for g in _hg_ref_rows(lvl, d)], axis=0)
        neg_abs = pltpu.bitcast(pltpu.bitcast(b - ref, jnp.uint32) | jnp.uint32(0x80000000), F32)
        xs.append(base * jnp.exp2(neg_abs))
    return dict(q=q, kk=kk, v=v, xs=xs, q_in=q * jnp.exp2(b), k_out=kk * jnp.exp2(b_end - b),
                bend=jnp.exp2(b_end))


def _hg_kernel(qf_ref, zf_ref, vf_ref, qb_ref, zb_ref, vb_ref, lb_ref, nm_ref, mk_ref,
               of_ref, ob_ref, sf_scr, sb_scr):
    @pl.when(pl.program_id(2) == 0)
    def _():
        sf_scr[...] = jnp.zeros_like(sf_scr)
        sb_scr[...] = jnp.zeros_like(sb_scr)

    for hh in range(HG_HPS):
        _hg_head(hh, (qf_ref, zf_ref, vf_ref), (qb_ref, zb_ref, vb_ref), lb_ref, nm_ref, mk_ref,
                 (of_ref, ob_ref), (sf_scr, sb_scr))


def _hg_head(hh, ins_f, ins_b, lb_ref, nm_ref, mk_ref, o_refs, st_refs):
    c = HG_TILE
    halves = (slice(0, HG_HALF), slice(HG_HALF, c))
    lanes = slice(hh * HG_HEAD_DIM, (hh + 1) * HG_HEAD_DIM)
    ins = (ins_f, ins_b)
    g = [_hg_gates(ins[d][0][0, :, lanes], ins[d][1][0, :, lanes], ins[d][2][0, :, lanes],
                   lb_ref[d:d + 1, lanes], nm_ref, d) for d in range(2)]
    scores = [[[], []] for _ in range(2)]
    for d in range(2):
        qb = g[d]["q"].astype(BF16)
        kb = g[d]["kk"].astype(BF16)
        for i, rs in enumerate(halves):
            scores[d][i].append((_dot_nt(qb[rs], kb[rs]), 0))
        for lvl in range(HG_LEVELS - 1):
            x = g[d]["xs"][lvl].astype(BF16)
            for i, rs in enumerate(halves):
                scores[d][i].append((_dot_nt(x[rs], x[rs]), 1 + lvl))
        x = g[d]["xs"][HG_LEVELS - 1].astype(BF16)
        late, early = (1, 0) if d == 0 else (0, 1)
        scores[d][late].append((_dot_nt(x[halves[late]], x[halves[early]]), None))
    outs = []
    for d in range(2):
        vb = g[d]["v"].astype(BF16)
        late, early = (1, 0) if d == 0 else (0, 1)
        acc = []
        for i, rs in enumerate(halves):
            sm = [s.astype(BF16) * mk_ref[d, m] if m is not None else s.astype(BF16) for s, m in scores[d][i]]
            vs = [vb[rs]] * HG_LEVELS + ([vb[halves[early]]] if i == late else [])
            acc.append(jnp.dot(jnp.concatenate(sm, axis=1), jnp.concatenate(vs, axis=0),
                               preferred_element_type=F32))
        outs.append(acc)
    for d in range(2):
        st = st_refs[d][hh]
        o_refs[d][0, :, lanes] = (jnp.concatenate(outs[d], axis=0)
                                  + _dot_nt(g[d]["q_in"].astype(BF16), st.astype(BF16))).astype(o_refs[d].dtype)
        vt = jnp.transpose(g[d]["v"]).astype(BF16)
        st_refs[d][hh] = st * g[d]["bend"] + jnp.dot(vt, g[d]["k_out"].astype(BF16),
                                                     preferred_element_type=F32)


def _hg_scan(p3, lb, nm_bf, mk_bf):
    bsz, seq_len, _ = p3.shape
    assert seq_len % HG_TILE == 0 and HG_HEADS % HG_HPS == 0
    tt = HG_TILE
    n_t = seq_len // tt
    hw = HG_HPS * HG_HEAD_DIM
    cpb = W_BRANCH // hw

    def fspec(col):
        return pl.BlockSpec((1, tt, hw), lambda b, h, k: (b, k, col * cpb + h))

    def bspec(col):
        return pl.BlockSpec((1, tt, hw), lambda b, h, k: (b, n_t - 1 - k, col * cpb + h))

    out = jax.ShapeDtypeStruct((bsz, seq_len, W_BRANCH), BF16)
    return pl.pallas_call(
        _hg_kernel,
        grid=(bsz, HG_HEADS // HG_HPS, n_t),
        in_specs=[fspec(COL_C_Q), fspec(COL_C_FF), fspec(COL_C_I),
                  bspec(COL_C_Q), bspec(COL_C_FB), bspec(COL_C_I),
                  pl.BlockSpec((2, hw), lambda b, h, k: (0, h)),
                  pl.BlockSpec((2, HG_NROWS, HG_TILE), lambda b, h, k: (0, 0, 0)),
                  pl.BlockSpec((2, HG_LEVELS, HG_HALF, HG_HALF), lambda b, h, k: (0, 0, 0, 0))],
        out_specs=[pl.BlockSpec((1, tt, hw), lambda b, h, k: (b, k, h)),
                   pl.BlockSpec((1, tt, hw), lambda b, h, k: (b, n_t - 1 - k, h))],
        out_shape=[out, out],
        scratch_shapes=[pltpu.VMEM((HG_HPS, HG_HEAD_DIM, HG_HEAD_DIM), F32),
                        pltpu.VMEM((HG_HPS, HG_HEAD_DIM, HG_HEAD_DIM), F32)],
        compiler_params=_cparams(("parallel", "parallel", "arbitrary")),
        name="hgrn2_scan",
    )(p3, p3, p3, p3, p3, p3, lb, nm_bf, mk_bf)


def _hg_post_kernel(of_ref, ob_ref, gate_ref, g_ref, o_ref):
    for h in range(HG_HEADS):
        lanes = slice(h * HG_HEAD_DIM, (h + 1) * HG_HEAD_DIM)
        o = of_ref[0, :, lanes].astype(F32) + ob_ref[0, :, lanes].astype(F32)
        o = o * lax.rsqrt(jnp.mean(o * o, axis=-1, keepdims=True) + EPS)
        y = o * g_ref[:, lanes] * _silu(gate_ref[0, :, lanes])
        o_ref[0, :, lanes] = y.astype(o_ref.dtype)


def _hg_post(o_f, o_b, p3, norm_g, tt=512):
    bsz, seq_len, _ = p3.shape
    blk = pl.BlockSpec((1, tt, W_BRANCH), lambda b, t: (b, t, 0))
    return pl.pallas_call(
        _hg_post_kernel,
        grid=(bsz, seq_len // tt),
        in_specs=[blk, blk,
                  pl.BlockSpec((1, tt, W_BRANCH), lambda b, t: (b, t, COL_C_GATE)),
                  pl.BlockSpec((1, W_BRANCH), lambda b, t: (0, 0))],
        out_specs=blk,
        out_shape=jax.ShapeDtypeStruct((bsz, seq_len, W_BRANCH), BF16),
        compiler_params=_cparams(("parallel", "parallel")),
        name="hgrn2_post",
    )(o_f, o_b, p3, norm_g)


def _s5_prep_kernel(are_ref, aim_ref, ldt_ref, bre_ref, bim_ref, bbr_ref, bbi_ref, pwr_ref, pwi_ref):
    a_re = are_ref[...]
    a_im = aim_ref[...]
    dt = jnp.exp(ldt_ref[...])
    mag = jnp.exp(dt * a_re)
    ang = dt * a_im
    ab_re = mag * jnp.cos(ang)
    ab_im = mag * jnp.sin(ang)
    den = a_re * a_re + a_im * a_im
    x_ = ab_re - 1.0
    y_ = ab_im
    g_re = (x_ * a_re + y_ * a_im) / den
    g_im = (y_ * a_re - x_ * a_im) / den
    for c in range(S5_GROUP_CH):
        b_re = bre_ref[c]
        b_im = bim_ref[c]
        bbr_ref[c] = g_re * b_re - g_im * b_im
        bbi_ref[c] = g_re * b_im + g_im * b_re
    p_re = ab_re
    p_im = ab_im
    for i in range(S5_SUB):
        pwr_ref[i] = p_re
        pwi_ref[i] = p_im
        p_re, p_im = p_re * ab_re - p_im * ab_im, p_re * ab_im + p_im * ab_re


def _s5_prep(a_re, a_im, log_dt, b_re, b_im):
    n = 2 * S5_GROUPS
    sb = jax.ShapeDtypeStruct((S5_GROUP_CH, n, S5_STATE), F32)
    spw = jax.ShapeDtypeStruct((S5_SUB, n, S5_STATE), F32)
    return pl.pallas_call(
        _s5_prep_kernel,
        out_shape=[sb, sb, spw, spw],
        name="s5_prep",
    )(a_re.reshape(n, S5_STATE), a_im.reshape(n, S5_STATE), log_dt.reshape(n, 1),
      jnp.transpose(b_re.reshape(n, S5_STATE, S5_GROUP_CH), (2, 0, 1)),
      jnp.transpose(b_im.reshape(n, S5_STATE, S5_GROUP_CH), (2, 0, 1)))


def _s5_layout(bbr, bbi, pwr, pwi, c_re, c_im):
    eye = jnp.eye(S5_GPT, dtype=F32)

    def pw_tiles(x):
        return jnp.transpose(x.reshape(S5_SUB, 2, S5_LT, S5_SW), (1, 2, 0, 3))

    pw_t = jnp.stack([pw_tiles(pwr), pw_tiles(pwi)], axis=1)

    def b_bd(x):
        x = jnp.transpose(x, (1, 0, 2)).reshape(2, S5_LT, S5_GPT, S5_GROUP_CH, S5_STATE)
        bd = x[:, :, :, :, None, :] * eye[None, None, :, None, :, None]
        return bd.reshape(2, S5_LT, LANES, S5_SW)

    b_t = jnp.concatenate([b_bd(bbr), b_bd(bbi)], axis=-1)

    def c_bd(x):
        x = jnp.transpose(x.reshape(2, S5_LT, S5_GPT, S5_GROUP_CH, S5_STATE), (0, 1, 2, 4, 3))
        bd = x[:, :, :, :, None, :] * eye[None, None, :, None, :, None]
        return bd.reshape(2, S5_LT, S5_SW, LANES)

    pw_c = jnp.transpose(pw_t, (0, 1, 2, 4, 3))
    return pw_t, pw_c, b_t, c_bd(c_re), c_bd(-c_im)


def _s5_operators(prep, c_re, c_im):
    pw_t, pw_c, b_t, cre_t, cimn_t = _s5_layout(*prep, c_re, c_im)
    b8_t, kint_t, ca_t = _s5_block_prep(pw_t, pw_c, b_t, cre_t, cimn_t)
    return pw_t, b8_t, kint_t, ca_t


def _s5_perm():
    n = SUBLANES * SUBLANES
    r = np.arange(n)
    p = np.zeros((n, n), np.float32)
    p[r, (r % SUBLANES) * SUBLANES + r // SUBLANES] = 1.0
    return p


def _dot3(x, y):
    xh = x.astype(BF16)
    xl = (x - xh.astype(F32)).astype(BF16)
    yh = y.astype(BF16)
    yl = (y - yh.astype(F32)).astype(BF16)
    return (jnp.dot(xh, yh, preferred_element_type=F32) + jnp.dot(xh, yl, preferred_element_type=F32)
            + jnp.dot(xl, yh, preferred_element_type=F32))


def _s5_block_kernel(b_ref, cre_ref, cimn_ref, pw_ref, pwc_ref, b8_ref, kint_ref, ca_ref):
    n = S5_BLK
    for d in range(2):
        b = b_ref[d, 0]
        cre = cre_ref[d, 0]
        cimn = cimn_ref[d, 0]
        cfull = jnp.concatenate([cre, cimn], axis=0)

        def b_pow(k, d=d, b=b):
            if k == 0:
                return b
            pr = pw_ref[d, 0, 0, k - 1:k, :]
            pi = pw_ref[d, 1, 0, k - 1:k, :]
            bre, bim = b[:, :S5_SW], b[:, S5_SW:]
            return jnp.concatenate([bre * pr - bim * pi, bre * pi + bim * pr], axis=1)

        def c_pow(k, d=d, cre=cre, cimn=cimn):
            ar = pwc_ref[d, 0, 0, :, k - 1:k]
            ai = pwc_ref[d, 1, 0, :, k - 1:k]
            return jnp.concatenate([cre * ar + cimn * ai, cimn * ar - cre * ai], axis=0)

        bp = [b_pow(k) for k in range(n)]
        kk = [_dot3(bp[k], cfull).astype(BF16) for k in range(n)]
        zero = jnp.zeros((LANES, LANES), BF16)
        for s in range(n):
            after = n - 1 - s if d == 0 else s
            b8_ref[d, 0, s * LANES:(s + 1) * LANES, :] = bp[after].astype(BF16)
            upto = s + 1 if d == 0 else n - s
            ca_ref[d, 0, :, s * LANES:(s + 1) * LANES] = c_pow(upto).astype(BF16)
            for r in range(n):
                lag = r - s if d == 0 else s - r
                kint_ref[d, 0, s * LANES:(s + 1) * LANES, r * LANES:(r + 1) * LANES] = kk[lag] if lag >= 0 else zero


def _s5_block_prep(pw_t, pw_c, b_t, cre_t, cimn_t):
    w = S5_BLK * LANES
    out = jax.ShapeDtypeStruct((2, S5_LT, w, w), BF16)
    out_b8 = jax.ShapeDtypeStruct((2, S5_LT, w, 2 * S5_SW), BF16)
    out_ca = jax.ShapeDtypeStruct((2, S5_LT, 2 * S5_SW, w), BF16)
    return pl.pallas_call(
        _s5_block_kernel,
        grid=(S5_LT,),
        in_specs=[pl.BlockSpec((2, 1, LANES, 2 * S5_SW), lambda l: (0, l, 0, 0)),
                  pl.BlockSpec((2, 1, S5_SW, LANES), lambda l: (0, l, 0, 0)),
                  pl.BlockSpec((2, 1, S5_SW, LANES), lambda l: (0, l, 0, 0)),
                  pl.BlockSpec((2, 2, 1, S5_SUB, S5_SW), lambda l: (0, 0, l, 0, 0)),
                  pl.BlockSpec((2, 2, 1, S5_SW, S5_SUB), lambda l: (0, 0, l, 0, 0))],
        out_specs=[pl.BlockSpec((2, 1, w, 2 * S5_SW), lambda l: (0, l, 0, 0)),
                   pl.BlockSpec((2, 1, w, w), lambda l: (0, l, 0, 0)),
                   pl.BlockSpec((2, 1, 2 * S5_SW, w), lambda l: (0, l, 0, 0))],
        out_shape=[out_b8, out, out_ca],
        compiler_params=_cparams(("parallel",)),
        name="s5_block_prep",
    )(b_t, cre_t, cimn_t, pw_t, pw_c)


def _vreg_grid_t(x):
    g = SUBLANES
    return jnp.concatenate(
        [jnp.concatenate([x[q * g:(q + 1) * g, p * LANES:(p + 1) * LANES] for q in range(g)], axis=1)
         for p in range(g)], axis=0)


def _s5_to_wide(x):
    g = SUBLANES
    return jnp.concatenate(
        [jnp.concatenate([x[(p * g + q) * g:(p * g + q + 1) * g, :] for q in range(g)], axis=1) for p in range(g)],
        axis=0)


def _s5_from_wide(x):
    g = SUBLANES
    return jnp.concatenate([x[p * g:(p + 1) * g, q * LANES:(q + 1) * LANES] for p in range(g) for q in range(g)],
                           axis=0)


def _s5_kernel(u_ref, pw_ref, b8_ref, kint_ref, ca_ref, d_ref, pm_ref, o_ref,
               drv_scr, y8_scr, hin_scr, *, seq_len):
    n_g = seq_len // (S5_TILE * S5_GROUP)
    nb = S5_SUB // S5_BLK
    rpt = SUBLANES * nb
    w = SUBLANES * LANES
    o_ref[0] = u_ref[0] * d_ref[...]
    zero_row = jnp.zeros((1, S5_SW), F32)
    zero_blk = jnp.zeros((SUBLANES, S5_SW), F32)

    def power(d, k):
        return pw_ref[d, 0, 0, k - 1:k, :], pw_ref[d, 1, 0, k - 1:k, :]

    def tile_rows(kg, i, d):
        t = kg * S5_GROUP + i
        kt = t if d == 0 else n_g * S5_GROUP - 1 - t
        return pl.ds(pl.multiple_of(kt * S5_TILE, S5_TILE), S5_TILE)

    def scan_tile(d, i, carry_re, carry_im):
        order = list(range(nb)) if d == 0 else list(range(nb - 1, -1, -1))
        ar, ai = (jnp.broadcast_to(x, (SUBLANES, S5_SW)) for x in power(d, S5_BLK))
        hr, hi = zero_blk, zero_blk
        loc = {}
        for m in order:
            rows = slice(i * rpt + m * SUBLANES, i * rpt + (m + 1) * SUBLANES)
            hr, hi = (ar * hr - ai * hi + drv_scr[d, rows, :S5_SW], ar * hi + ai * hr + drv_scr[d, rows, S5_SW:])
            loc[m] = (hr, hi)
        asr, asi = power(d, S5_SUB)
        c_re, c_im = carry_re, carry_im
        ent_re, ent_im = [None] * SUBLANES, [None] * SUBLANES
        for j in (range(SUBLANES) if d == 0 else range(SUBLANES - 1, -1, -1)):
            ent_re[j], ent_im[j] = c_re, c_im
            er, ei = hr[j:j + 1, :], hi[j:j + 1, :]
            c_re, c_im = er + asr * c_re - asi * c_im, ei + asr * c_im + asi * c_re
        ent_re = jnp.concatenate(ent_re, axis=0)
        ent_im = jnp.concatenate(ent_im, axis=0)
        prev_re, prev_im = {order[0]: ent_re}, {order[0]: ent_im}
        for idx, m in enumerate(order[:-1]):
            pr, pi = (jnp.broadcast_to(x, (SUBLANES, S5_SW)) for x in power(d, S5_BLK * (idx + 1)))
            prev_re[order[idx + 1]] = loc[m][0] + pr * ent_re - pi * ent_im
            prev_im[order[idx + 1]] = loc[m][1] + pr * ent_im + pi * ent_re
        h_in = jnp.concatenate([jnp.concatenate([prev_re[m] for m in range(nb)], axis=0),
                                jnp.concatenate([prev_im[m] for m in range(nb)], axis=0)], axis=1)
        hin_scr[d, i * rpt:(i + 1) * rpt, :] = h_in.astype(BF16)
        return c_re, c_im

    def group_body(kg, carry):
        for d in range(2):
            wide = jnp.concatenate([_s5_to_wide(u_ref[0, tile_rows(kg, i, d), :]) for i in range(S5_GROUP)], axis=1)
            perm = jnp.dot(pm_ref[...], wide.astype(BF16), preferred_element_type=F32)
            u8 = jnp.concatenate([_vreg_grid_t(perm[:, i * w:(i + 1) * w]) for i in range(S5_GROUP)],
                                 axis=0).astype(BF16)
            drv_scr[d] = jnp.dot(u8, b8_ref[d, 0], preferred_element_type=F32)
            y8_scr[d] = jnp.dot(u8, kint_ref[d, 0], preferred_element_type=F32)
        new_carry = []
        for d in range(2):
            c_re, c_im = carry[2 * d], carry[2 * d + 1]
            for i in range(S5_GROUP):
                c_re, c_im = scan_tile(d, i, c_re, c_im)
            new_carry += [c_re, c_im]
        for d in range(2):
            y = y8_scr[d] + jnp.dot(hin_scr[d], ca_ref[d, 0], preferred_element_type=F32)
            yt = jnp.concatenate([_vreg_grid_t(y[i * rpt:(i + 1) * rpt]) for i in range(S5_GROUP)], axis=1)
            y_hi = yt.astype(BF16)
            y_lo = (yt - y_hi.astype(F32)).astype(BF16)
            yn = (jnp.dot(pm_ref[...], y_hi, preferred_element_type=F32)
                  + jnp.dot(pm_ref[...], y_lo, preferred_element_type=F32))
            for i in range(S5_GROUP):
                rows = tile_rows(kg, i, d)
                o_ref[0, rows, :] = o_ref[0, rows, :] + _s5_from_wide(yn[:, i * w:(i + 1) * w])
        return tuple(new_carry)

    lax.fori_loop(0, n_g, group_body, (zero_row,) * 4)


def _s5_scan(p3, pw_t, b8_t, kint_t, ca_t, d_skip, perm):
    bsz, seq_len, _ = p3.shape
    assert seq_len % (S5_TILE * S5_GROUP) == 0 and S5_BLK == SUBLANES == S5_SUB // S5_BLK
    cpb = W_BRANCH // LANES
    w = S5_BLK * LANES
    rows = S5_GROUP * SUBLANES * (S5_SUB // S5_BLK)
    return pl.pallas_call(
        functools.partial(_s5_kernel, seq_len=seq_len),
        grid=(S5_LT, bsz),
        in_specs=[pl.BlockSpec((1, seq_len, LANES), lambda l, b: (b, 0, COL_D_IN * cpb + l)),
                  pl.BlockSpec((2, 2, 1, S5_SUB, S5_SW), lambda l, b: (0, 0, l, 0, 0)),
                  pl.BlockSpec((2, 1, w, 2 * S5_SW), lambda l, b: (0, l, 0, 0)),
                  pl.BlockSpec((2, 1, w, w), lambda l, b: (0, l, 0, 0)),
                  pl.BlockSpec((2, 1, 2 * S5_SW, w), lambda l, b: (0, l, 0, 0)),
                  pl.BlockSpec((1, LANES), lambda l, b: (0, l)),
                  pl.BlockSpec((SUBLANES * SUBLANES, SUBLANES * SUBLANES), lambda l, b: (0, 0))],
        out_specs=pl.BlockSpec((1, seq_len, LANES), lambda l, b: (b, 0, l)),
        out_shape=jax.ShapeDtypeStruct((bsz, seq_len, W_BRANCH), F32),
        scratch_shapes=[pltpu.VMEM((2, rows, 2 * S5_SW), F32),
                        pltpu.VMEM((2, rows, w), F32),
                        pltpu.VMEM((2, rows, 2 * S5_SW), BF16)],
        compiler_params=_cparams(("parallel", "parallel")),
        name="s5_scan",
    )(p3, pw_t, b8_t, kint_t, ca_t, d_skip, perm)


def _s5_post_kernel(y_ref, gate_ref, w_ref, b_ref, o_ref):
    y = y_ref[0]
    c0 = math.sqrt(2.0 / math.pi)
    z = 0.5 * y * (1.0 + jnp.tanh(c0 * (y + 0.044715 * (y * y * y))))
    lin = jnp.dot(z.astype(BF16), w_ref[...], preferred_element_type=F32) + b_ref[...]
    out = z * _sigmoid(lin) * _silu(gate_ref[0])
    o_ref[0] = out.astype(o_ref.dtype)


def _s5_post(y, p3, glu_w_bf, glu_b, tt=512):
    bsz, seq_len, _ = p3.shape
    blk = pl.BlockSpec((1, tt, W_BRANCH), lambda b, t: (b, t, 0))
    return pl.pallas_call(
        _s5_post_kernel,
        grid=(bsz, seq_len // tt),
        in_specs=[blk,
                  pl.BlockSpec((1, tt, W_BRANCH), lambda b, t: (b, t, COL_D_GATE)),
                  pl.BlockSpec((W_BRANCH, W_BRANCH), lambda b, t: (0, 0)),
                  pl.BlockSpec((1, W_BRANCH), lambda b, t: (0, 0))],
        out_specs=blk,
        out_shape=jax.ShapeDtypeStruct((bsz, seq_len, W_BRANCH), BF16),
        compiler_params=_cparams(("parallel", "parallel")),
        name="s5_post",
    )(y, p3, glu_w_bf, glu_b)


def _merge_kernel(ya_ref, yb_ref, yc_ref, yd_ref, r0_ref, r1_ref, r2_ref, r3_ref, w_ref, o_ref):
    acc = None
    for i, (y_ref, r_ref) in enumerate(((ya_ref, r0_ref), (yb_ref, r1_ref), (yc_ref, r2_ref), (yd_ref, r3_ref))):
        term = _sigmoid(r_ref[...].astype(F32)) * jnp.dot(y_ref[...], w_ref[i], preferred_element_type=F32)
        acc = term if acc is None else acc + term
    o_ref[...] = acc.astype(o_ref.dtype)


def _merge(ys, r2, w_stack_bf, tm=512, tn=1024):
    t = r2.shape[0]
    npb = D_MODEL // tn
    yspec = pl.BlockSpec((tm, W_BRANCH), lambda j, i: (i, 0))

    def rspec(br):
        return pl.BlockSpec((tm, tn), lambda j, i: (i, br * npb + j))

    return pl.pallas_call(
        _merge_kernel,
        grid=(npb, t // tm),
        in_specs=[yspec, yspec, yspec, yspec, rspec(0), rspec(1), rspec(2), rspec(3),
                  pl.BlockSpec((N_BRANCH, W_BRANCH, tn), lambda j, i: (0, 0, j))],
        out_specs=pl.BlockSpec((tm, tn), lambda j, i: (i, j)),
        out_shape=jax.ShapeDtypeStruct((t, D_MODEL), BF16),
        compiler_params=_cparams(("parallel", "parallel")),
        name="merge",
    )(*ys, r2, r2, r2, r2, w_stack_bf)


def _out_proj_kernel(x_ref, m_ref, w_ref, g_ref, o_ref, *, final_norm):
    y = x_ref[...] + jnp.dot(m_ref[...], w_ref[...], preferred_element_type=F32)
    if final_norm:
        y = y * lax.rsqrt(jnp.mean(y * y, axis=-1, keepdims=True) + EPS) * g_ref[...]
    o_ref[...] = y


def _out_proj(x2, m, w_bf, final_g, final_norm, tm=512):
    t = x2.shape[0]
    blk = pl.BlockSpec((tm, D_MODEL), lambda i: (i, 0))
    return pl.pallas_call(
        functools.partial(_out_proj_kernel, final_norm=final_norm),
        grid=(t // tm,),
        in_specs=[blk, blk,
                  pl.BlockSpec((D_MODEL, D_MODEL), lambda i: (0, 0)),
                  pl.BlockSpec((1, D_MODEL), lambda i: (0, 0))],
        out_specs=blk,
        out_shape=jax.ShapeDtypeStruct((t, D_MODEL), F32),
        compiler_params=_cparams(("parallel",)),
        name="out_proj",
    )(x2, m, w_bf, final_g)


def _lb_kernel(x_ref, o_ref):
    x = x_ref[...]
    e = jnp.exp(x - jnp.max(x, axis=0, keepdims=True))
    sm = e / jnp.sum(e, axis=0, keepdims=True)
    run = jnp.zeros_like(sm[0])
    for l in range(DEPTH):
        run = run + sm[l]
        o_ref[l] = run - sm[0]


def _lower_bounds(hg_lb):
    return pl.pallas_call(
        _lb_kernel,
        out_shape=jax.ShapeDtypeStruct(hg_lb.shape, F32),
        name="hg_lower_bounds",
    )(hg_lb)


def _layer(x, lw, final_g, final_norm):
    bsz, seq_len, _ = x.shape
    x2 = x.reshape(bsz * seq_len, D_MODEL)
    p2, r2 = _in_proj(x2, lw["norm_g"], lw["w_in"])
    p3 = p2.reshape(bsz, seq_len, COL_R * W_BRANCH)
    ya = _conv_branch(p3, lw["conv_w"], lw["conv_b"], lw["conv_ln_g"], lw["conv_ln_b"])
    yb = _pool_branch(p3, lw["pool_w"], lw["pool_scale"])
    o_f, o_b = _hg_scan(p3, lw["lb"], lw["hg_nm"], lw["hg_mk"])
    yc = _hg_post(o_f, o_b, p3, lw["hg_norm_g"])
    y5 = _s5_scan(p3, *lw["s5"], lw["s5_d"], lw["s5_perm"])
    yd = _s5_post(y5, p3, lw["s5_glu_w"], lw["s5_glu_b"])
    ys = [y.reshape(bsz * seq_len, W_BRANCH) for y in (ya, yb, yc, yd)]
    m = _merge(ys, r2, lw["w_branch_out"])
    out = _out_proj(x2, m, lw["w_out"], final_g, final_norm)
    return out.reshape(bsz, seq_len, D_MODEL)


def kernel(x_prompt, x_sample, norm_g, w_in, conv_w, conv_b, conv_ln_g, conv_ln_b, w_a_out, pool_w, pool_scale, w_b_out, hg_lb, hg_norm_g, w_c_out, s5_a_re, s5_a_im, s5_log_dt, s5_b_re, s5_b_im, s5_c_re, s5_c_im, s5_d, s5_glu_w, s5_glu_b, w_d_out, w_out, final_g):
    lb_all = _lower_bounds(hg_lb)
    nm, mk = _hg_tables()
    nm_bf = jnp.asarray(nm, BF16)
    mk_bf = jnp.asarray(mk, BF16)
    perm_bf = jnp.asarray(_s5_perm(), BF16)
    layers = []
    for l in range(DEPTH):
        prep = _s5_prep(s5_a_re[l], s5_a_im[l], s5_log_dt[l], s5_b_re[l], s5_b_im[l])
        layers.append(dict(
            norm_g=norm_g[l][None], w_in=w_in[l].astype(BF16),
            conv_w=conv_w[l], conv_b=conv_b[l][None], conv_ln_g=conv_ln_g[l][None], conv_ln_b=conv_ln_b[l][None],
            pool_w=pool_w[l].astype(BF16), pool_scale=pool_scale[l][None],
            lb=lb_all[l], hg_nm=nm_bf, hg_mk=mk_bf, hg_norm_g=hg_norm_g[l][None],
            s5=_s5_operators(prep, s5_c_re[l], s5_c_im[l]), s5_d=s5_d[l][None],
            s5_perm=perm_bf,
            s5_glu_w=s5_glu_w[l].astype(BF16), s5_glu_b=s5_glu_b[l][None],
            w_branch_out=jnp.stack([w_a_out[l], w_b_out[l], w_c_out[l], w_d_out[l]]).astype(BF16),
            w_out=w_out[l].astype(BF16)))
    fg = final_g[None]

    def run(x):
        for l in range(DEPTH):
            x = _layer(x, layers[l], fg, l == DEPTH - 1)
        return x

    return (run(x_prompt), run(x_sample))
```

```python
import functools
import math

import numpy as np
import jax
import jax.numpy as jnp
from jax import lax
from jax.experimental import pallas as pl
from jax.experimental.pallas import tpu as pltpu

F32 = jnp.float32
BF16 = jnp.bfloat16

D_MODEL = 2048
DEPTH = 2
W_BRANCH = 1024
N_BRANCH = 4
N_IN = 12 * W_BRANCH + N_BRANCH * D_MODEL
CONV_WIDTH = 31
CONV_PAD = CONV_WIDTH // 2
POOL_WINDOWS = (2, 4, 8, 16)
POOL_GROUP = W_BRANCH // len(POOL_WINDOWS)
HG_HEAD_DIM = 128
HG_HEADS = W_BRANCH // HG_HEAD_DIM
S5_GROUP_CH = 16
S5_GROUPS = W_BRANCH // S5_GROUP_CH
S5_STATE = 64
EPS = 1e-6

LANES = 128
SUBLANES = 8
VMEM_LIMIT = 52 * 1024 * 1024

COL_A_VAL, COL_A_GLU, COL_A_GATE, COL_B_IN, COL_B_GATE = 0, 1, 2, 3, 4
COL_C_Q, COL_C_FF, COL_C_FB, COL_C_I, COL_C_GATE, COL_D_IN, COL_D_GATE = 5, 6, 7, 8, 9, 10, 11
COL_R = 12

HALO = 16
CONV_LANES = 256
HG_TILE = 256
HG_HALF = HG_TILE // 2
HG_HPS = 8
HG_LEVELS = 8
HG_NROWS = 4 * HG_TILE
S5_TILE = 512
S5_SUB = S5_TILE // SUBLANES
S5_BLK = 8
S5_GROUP = 4
S5_LT = W_BRANCH // LANES
S5_GPT = LANES // S5_GROUP_CH
S5_SW = S5_GPT * S5_STATE


def _sigmoid(x):
    return 1.0 / (1.0 + jnp.exp(-x))


def _silu(x):
    return x * _sigmoid(x)


def _cparams(sem):
    return pltpu.CompilerParams(dimension_semantics=sem, vmem_limit_bytes=VMEM_LIMIT)


def _in_proj_kernel(x_ref, g_ref, w_ref, o_ref, r_ref, h_ref, *, n_main):
    j = pl.program_id(1)

    @pl.when(j == 0)
    def _():
        x = x_ref[...]
        ms = jnp.mean(x * x, axis=-1, keepdims=True)
        h_ref[...] = (x * lax.rsqrt(ms + EPS) * g_ref[...]).astype(BF16)

    @pl.when(j < n_main)
    def _():
        o_ref[...] = jnp.dot(h_ref[...], w_ref[...], preferred_element_type=F32)

    @pl.when(j >= n_main)
    def _():
        r_ref[...] = jnp.dot(h_ref[...], w_ref[...], preferred_element_type=F32).astype(BF16)


def _in_proj(x2, g, w_bf, tm=1024, tn=1024):
    t = x2.shape[0]
    n_main = COL_R * W_BRANCH // tn
    return pl.pallas_call(
        functools.partial(_in_proj_kernel, n_main=n_main),
        grid=(t // tm, N_IN // tn),
        in_specs=[pl.BlockSpec((tm, D_MODEL), lambda i, j: (i, 0)),
                  pl.BlockSpec((1, D_MODEL), lambda i, j: (0, 0)),
                  pl.BlockSpec((D_MODEL, tn), lambda i, j: (0, j))],
        out_specs=[pl.BlockSpec((tm, tn), lambda i, j: (i, jnp.minimum(j, n_main - 1))),
                   pl.BlockSpec((tm, tn), lambda i, j: (i, jnp.maximum(j - n_main, 0)))],
        out_shape=[jax.ShapeDtypeStruct((t, COL_R * W_BRANCH), F32),
                   jax.ShapeDtypeStruct((t, N_BRANCH * D_MODEL), BF16)],
        scratch_shapes=[pltpu.VMEM((tm, D_MODEL), BF16)],
        compiler_params=_cparams(("parallel", "arbitrary")),
        name="in_proj",
    )(x2, g, w_bf)


def _conv_kernel(main_ref, gate_ref, prev_ref, next_ref, cw_ref, cb_ref, lg_ref, lb_ref, o_ref,
                 us_scr, wb_scr, *, tt, rc):
    t = pl.program_id(1)
    n_t = pl.num_programs(1)

    def glu(ref):
        return ref[0, :, :W_BRANCH] * _sigmoid(ref[0, :, W_BRANCH:])

    us_scr[0, HALO:HALO + tt, :] = glu(main_ref)
    us_scr[0, 0:HALO, :] = jnp.where(t > 0, glu(prev_ref), 0.0)
    us_scr[0, HALO + tt:2 * HALO + tt, :] = jnp.where(t < n_t - 1, glu(next_ref), 0.0)
    n = tt + 2 * HALO - SUBLANES
    for s in range(1, SUBLANES):
        us_scr[s, 0:n, :] = us_scr[0, s:s + n, :]
    for j in range(CONV_WIDTH):
        wb_scr[j] = jnp.broadcast_to(cw_ref[j:j + 1, :], (SUBLANES, W_BRANCH))

    def body(c, carry):
        r0 = pl.multiple_of(c * rc, rc)
        parts = []
        for lc in range(W_BRANCH // CONV_LANES):
            lanes = slice(lc * CONV_LANES, (lc + 1) * CONV_LANES)
            acc = jnp.zeros((rc // SUBLANES, SUBLANES, CONV_LANES), F32)
            for j in range(CONV_WIDTH):
                off = HALO - CONV_PAD + j
                s = off % SUBLANES
                tap = us_scr[s, pl.ds(r0 + (off - s), rc), lanes]
                acc = acc + tap.reshape(rc // SUBLANES, SUBLANES, CONV_LANES) * wb_scr[j, :, lanes]
            parts.append(acc.reshape(rc, CONV_LANES))
        y = jnp.concatenate(parts, axis=-1) + cb_ref[...]
        mu = jnp.mean(y, axis=-1, keepdims=True)
        yc = y - mu
        var = jnp.mean(yc * yc, axis=-1, keepdims=True)
        yn = yc * lax.rsqrt(var + EPS) * lg_ref[...] + lb_ref[...]
        out = _silu(yn) * _silu(gate_ref[0, pl.ds(r0, rc), :])
        o_ref[0, pl.ds(r0, rc), :] = out.astype(o_ref.dtype)
        return carry

    lax.fori_loop(0, tt // rc, body, 0)


def _halo_specs(tt, width, col_blk, seq_len):
    per = tt // HALO
    last = seq_len // HALO - 1
    prev = pl.BlockSpec((1, HALO, width), lambda b, t: (b, jnp.maximum(t * per - 1, 0), col_blk))
    nxt = pl.BlockSpec((1, HALO, width), lambda b, t: (b, jnp.minimum((t + 1) * per, last), col_blk))
    return prev, nxt


def _conv_branch(p3, cw, cb, lg, lb, tt=512, rc=64):
    bsz, seq_len, _ = p3.shape
    prev, nxt = _halo_specs(tt, 2 * W_BRANCH, 0, seq_len)
    vec = pl.BlockSpec((1, W_BRANCH), lambda b, t: (0, 0))
    return pl.pallas_call(
        functools.partial(_conv_kernel, tt=tt, rc=rc),
        grid=(bsz, seq_len // tt),
        in_specs=[pl.BlockSpec((1, tt, 2 * W_BRANCH), lambda b, t: (b, t, 0)),
                  pl.BlockSpec((1, tt, W_BRANCH), lambda b, t: (b, t, COL_A_GATE)),
                  prev, nxt,
                  pl.BlockSpec((CONV_WIDTH, W_BRANCH), lambda b, t: (0, 0)),
                  vec, vec, vec],
        out_specs=pl.BlockSpec((1, tt, W_BRANCH), lambda b, t: (b, t, 0)),
        out_shape=jax.ShapeDtypeStruct((bsz, seq_len, W_BRANCH), BF16),
        scratch_shapes=[pltpu.VMEM((SUBLANES, tt + 2 * HALO, W_BRANCH), F32),
                        pltpu.VMEM((CONV_WIDTH, SUBLANES, W_BRANCH), F32)],
        compiler_params=_cparams(("parallel", "parallel")),
        name="conv_branch",
    )(p3, p3, p3, p3, cw, cb, lg, lb)


def _pool_kernel(main_ref, gate_ref, prev_ref, next_ref, pw_ref, ps_ref, o_ref, e_scr, s_scr,
                 *, tt, seq_len):
    t = pl.program_id(1)
    n_t = pl.num_programs(1)
    rows = tt + 2 * HALO
    e_scr[HALO:HALO + tt, :] = main_ref[0]
    e_scr[0:HALO, :] = jnp.where(t > 0, prev_ref[0], 0.0)
    e_scr[HALO + tt:rows, :] = jnp.where(t < n_t - 1, next_ref[0], 0.0)

    tpos = t * tt + lax.broadcasted_iota(jnp.int32, (tt, 1), 0)
    for g, win in enumerate(POOL_WINDOWS):
        lanes = slice(g * POOL_GROUP, (g + 1) * POOL_GROUP)
        n = rows - 2
        s_scr[1:1 + n, lanes] = e_scr[1:1 + n, lanes] + e_scr[0:n, lanes]
        half = 1
        while 2 * half < win:
            n = rows - 2 * half - 2 * half
            lo = 2 * half
            a = s_scr[lo + half:lo + half + n, lanes]
            b = s_scr[lo - half:lo - half + n, lanes]
            s_scr[lo:lo + n, lanes] = a + b
            half *= 2
        left = win // 2
        right = win - 1 - left
        cnt = (jnp.minimum(tpos + right, seq_len - 1) + 1 - jnp.maximum(tpos - left, 0)).astype(F32)
        u = e_scr[HALO:HALO + tt, lanes]
        d = s_scr[HALO:HALO + tt, lanes] / cnt - u
        y = jnp.dot(d.astype(BF16), pw_ref[g], preferred_element_type=F32)
        y = y * ps_ref[:, lanes] * _silu(gate_ref[0, :, lanes])
        o_ref[0, :, lanes] = y.astype(o_ref.dtype)


def _pool_branch(p3, pw_bf, ps, tt=512):
    bsz, seq_len, _ = p3.shape
    prev, nxt = _halo_specs(tt, W_BRANCH, COL_B_IN, seq_len)
    return pl.pallas_call(
        functools.partial(_pool_kernel, tt=tt, seq_len=seq_len),
        grid=(bsz, seq_len // tt),
        in_specs=[pl.BlockSpec((1, tt, W_BRANCH), lambda b, t: (b, t, COL_B_IN)),
                  pl.BlockSpec((1, tt, W_BRANCH), lambda b, t: (b, t, COL_B_GATE)),
                  prev, nxt,
                  pl.BlockSpec((len(POOL_WINDOWS), POOL_GROUP, POOL_GROUP), lambda b, t: (0, 0, 0)),
                  pl.BlockSpec((1, W_BRANCH), lambda b, t: (0, 0))],
        out_specs=pl.BlockSpec((1, tt, W_BRANCH), lambda b, t: (b, t, 0)),
        out_shape=jax.ShapeDtypeStruct((bsz, seq_len, W_BRANCH), BF16),
        scratch_shapes=[pltpu.VMEM((tt + 2 * HALO, W_BRANCH), F32),
                        pltpu.VMEM((tt + 2 * HALO, W_BRANCH), F32)],
        compiler_params=_cparams(("parallel", "parallel")),
        name="pool_branch",
    )(p3, p3, p3, p3, pw_bf, ps)


def _hg_tables():
    c = HG_TILE
    t = np.arange(c)
    rr = np.arange(c)[None, :]
    vstart = (t // SUBLANES) * SUBLANES
    pref = [t, vstart + 7, vstart + 3, vstart + np.where(t % SUBLANES < 4, 1, 5)]
    nmat = np.stack([rr <= p[:, None] for p in pref]).astype(np.float32)
    masks = np.zeros((HG_LEVELS, HG_HALF, HG_HALF), np.float32)
    masks[0] = np.eye(HG_HALF)
    th = np.arange(HG_HALF)
    for lvl in range(HG_LEVELS - 1):
        half = 1 << lvl
        bh = th // (2 * half)
        lh = (th % (2 * half)) >= half
        masks[1 + lvl] = (bh[:, None] == bh[None, :]) & lh[:, None] & (~lh)[None, :]
    nm = np.stack([nmat, nmat[:, ::-1, ::-1]]).reshape(2, HG_NROWS, c)
    mk = np.stack([masks, masks[:, ::-1, ::-1]])
    return nm, mk


def _dot_nt(a, b):
    return lax.dot_general(a, b, (((1,), (1,)), ((), ())), preferred_element_type=F32)


def _hg_ref_rows(lvl, d):
    out = []
    for m in range(HG_TILE // SUBLANES):
        block = 2 << lvl
        mid = (m * SUBLANES // block) * block + block // 2
        out.append(mid // SUBLANES - 1 if d == 0 else mid // SUBLANES)
    return out


def _hg_gates(q, z, v, lbv, nm_ref, d):
    c = HG_TILE
    sg = _sigmoid(z)
    f = lbv + (1.0 - lbv) * sg
    lf = jnp.log2(f)
    kk = (1.0 - lbv) * (1.0 - sg)
    hi = lf.astype(BF16)
    lo = (lf - hi.astype(F32)).astype(BF16)
    tab = jnp.dot(nm_ref[d], jnp.concatenate([hi, lo], axis=1), preferred_element_type=F32)
    tab = tab[:, :HG_HEAD_DIM] + tab[:, HG_HEAD_DIM:]
    b, b_grp, b_l2, b_l1 = (tab[i * c:(i + 1) * c] for i in range(4))
    b_end = b_grp[c - 1:c] if d == 0 else b_grp[0:1]
    n_g = c // SUBLANES
    sub = lax.broadcasted_iota(jnp.int32, (1, SUBLANES, HG_HEAD_DIM), 1)
    q3 = q.reshape(n_g, SUBLANES, HG_HEAD_DIM)
    k3 = kk.reshape(n_g, SUBLANES, HG_HEAD_DIM)
    xs = []
    for lvl in range(HG_LEVELS):
        if (1 << lvl) < SUBLANES:
            later = (jnp.right_shift(sub, lvl) & 1) == (1 - d)
            base = jnp.where(later, q3, k3).reshape(c, HG_HEAD_DIM)
        else:
            pick = [((m * SUBLANES) >> lvl) & 1 == (1 - d) for m in range(n_g)]
            base = jnp.concatenate([(q3 if p else k3)[m] for m, p in enumerate(pick)], axis=0)
        if lvl == 0:
            later0 = jnp.broadcast_to(later, (n_g, SUBLANES, HG_HEAD_DIM)).reshape(c, HG_HEAD_DIM)
            xs.append(jnp.where(later0, base * f, base))
            continue
        if lvl == 1:
            ref = b_l1
        elif lvl == 2:
            ref = b_l2
        else:
            ref = jnp.concatenate([b_grp[g * SUBLANES:(g + 1) * SUBLANES] for g in _hg_ref_rows(lvl, d)], axis=0)
        neg_abs = pltpu.bitcast(pltpu.bitcast(b - ref, jnp.uint32) | jnp.uint32(0x80000000), F32)
        xs.append(base * jnp.exp2(neg_abs))
    return dict(q=q, kk=kk, v=v, xs=xs, q_in=q * jnp.exp2(b), k_out=kk * jnp.exp2(b_end - b),
                bend=jnp.exp2(b_end))


def _hg_kernel(qf_ref, zf_ref, vf_ref, qb_ref, zb_ref, vb_ref, lb_ref, nm_ref, mk_ref,
               of_ref, ob_ref, sf_scr, sb_scr):
    @pl.when(pl.program_id(2) == 0)
    def _():
        sf_scr[...] = jnp.zeros_like(sf_scr)
        sb_scr[...] = jnp.zeros_like(sb_scr)

    for hh in range(HG_HPS):
        _hg_head(hh, (qf_ref, zf_ref, vf_ref), (qb_ref, zb_ref, vb_ref), lb_ref, nm_ref, mk_ref,
                 (of_ref, ob_ref), (sf_scr, sb_scr))


def _hg_head(hh, ins_f, ins_b, lb_ref, nm_ref, mk_ref, o_refs, st_refs):
    c = HG_TILE
    halves = (slice(0, HG_HALF), slice(HG_HALF, c))
    lanes = slice(hh * HG_HEAD_DIM, (hh + 1) * HG_HEAD_DIM)
    ins = (ins_f, ins_b)
    g = [_hg_gates(ins[d][0][0, :, lanes], ins[d][1][0, :, lanes], ins[d][2][0, :, lanes],
                   lb_ref[d:d + 1, lanes], nm_ref, d) for d in range(2)]
    scores = [[[], []] for _ in range(2)]
    for d in range(2):
        qb = g[d]["q"].astype(BF16)
        kb = g[d]["kk"].astype(BF16)
        for i, rs in enumerate(halves):
            scores[d][i].append((_dot_nt(qb[rs], kb[rs]), 0))
        for lvl in range(HG_LEVELS - 1):
            x = g[d]["xs"][lvl].astype(BF16)
            for i, rs in enumerate(halves):
                scores[d][i].append((_dot_nt(x[rs], x[rs]), 1 + lvl))
        x = g[d]["xs"][HG_LEVELS - 1].astype(BF16)
        late, early = (1, 0) if d == 0 else (0, 1)
        scores[d][late].append((_dot_nt(x[halves[late]], x[halves[early]]), None))
    outs = []
    for d in range(2):
        vb = g[d]["v"].astype(BF16)
        late, early = (1, 0) if d == 0 else (0, 1)
        acc = []
        for i, rs in enumerate(halves):
            sm = [s.astype(BF16) * mk_ref[d, m] if m is not None else s.astype(BF16) for s, m in scores[d][i]]
            vs = [vb[rs]] * HG_LEVELS + ([vb[halves[early]]] if i == late else [])
            acc.append(jnp.dot(jnp.concatenate(sm, axis=1), jnp.concatenate(vs, axis=0),
                               preferred_element_type=F32))
        outs.append(acc)
    for d in range(2):
        st = st_refs[d][hh]
        o_refs[d][0, :, lanes] = (jnp.concatenate(outs[d], axis=0)
                                  + _dot_nt(g[d]["q_in"].astype(BF16), st.astype(BF16))).astype(o_refs[d].dtype)
        vt = jnp.transpose(g[d]["v"]).astype(BF16)
        st_refs[d][hh] = st * g[d]["bend"] + jnp.dot(vt, g[d]["k_out"].astype(BF16),
                                                     preferred_element_type=F32)


def _hg_scan(p3, lb, nm_bf, mk_bf):
    bsz, seq_len, _ = p3.shape
    tt = HG_TILE
    n_t = seq_len // tt
    hw = HG_HPS * HG_HEAD_DIM
    cpb = W_BRANCH // hw

    def fspec(col):
        return pl.BlockSpec((1, tt, hw), lambda b, h, k: (b, k, col * cpb + h))

    def bspec(col):
        return pl.BlockSpec((1, tt, hw), lambda b, h, k: (b, n_t - 1 - k, col * cpb + h))

    out = jax.ShapeDtypeStruct((bsz, seq_len, W_BRANCH), BF16)
    return pl.pallas_call(
        _hg_kernel,
        grid=(bsz, HG_HEADS // HG_HPS, n_t),
        in_specs=[fspec(COL_C_Q), fspec(COL_C_FF), fspec(COL_C_I),
                  bspec(COL_C_Q), bspec(COL_C_FB), bspec(COL_C_I),
                  pl.BlockSpec((2, hw), lambda b, h, k: (0, h)),
                  pl.BlockSpec((2, HG_NROWS, HG_TILE), lambda b, h, k: (0, 0, 0)),
                  pl.BlockSpec((2, HG_LEVELS, HG_HALF, HG_HALF), lambda b, h, k: (0, 0, 0, 0))],
        out_specs=[pl.BlockSpec((1, tt, hw), lambda b, h, k: (b, k, h)),
                   pl.BlockSpec((1, tt, hw), lambda b, h, k: (b, n_t - 1 - k, h))],
        out_shape=[out, out],
        scratch_shapes=[pltpu.VMEM((HG_HPS, HG_HEAD_DIM, HG_HEAD_DIM), F32),
                        pltpu.VMEM((HG_HPS, HG_HEAD_DIM, HG_HEAD_DIM), F32)],
        compiler_params=_cparams(("parallel", "parallel", "arbitrary")),
        name="hgrn2_scan",
    )(p3, p3, p3, p3, p3, p3, lb, nm_bf, mk_bf)


def _hg_post_kernel(of_ref, ob_ref, gate_ref, g_ref, o_ref):
    for h in range(HG_HEADS):
        lanes = slice(h * HG_HEAD_DIM, (h + 1) * HG_HEAD_DIM)
        o = of_ref[0, :, lanes].astype(F32) + ob_ref[0, :, lanes].astype(F32)
        o = o * lax.rsqrt(jnp.mean(o * o, axis=-1, keepdims=True) + EPS)
        y = o * g_ref[:, lanes] * _silu(gate_ref[0, :, lanes])
        o_ref[0, :, lanes] = y.astype(o_ref.dtype)


def _hg_post(o_f, o_b, p3, norm_g, tt=512):
    bsz, seq_len, _ = p3.shape
    blk = pl.BlockSpec((1, tt, W_BRANCH), lambda b, t: (b, t, 0))
    return pl.pallas_call(
        _hg_post_kernel,
        grid=(bsz, seq_len // tt),
        in_specs=[blk, blk,
                  pl.BlockSpec((1, tt, W_BRANCH), lambda b, t: (b, t, COL_C_GATE)),
                  pl.BlockSpec((1, W_BRANCH), lambda b, t: (0, 0))],
        out_specs=blk,
        out_shape=jax.ShapeDtypeStruct((bsz, seq_len, W_BRANCH), BF16),
        compiler_params=_cparams(("parallel", "parallel")),
        name="hgrn2_post",
    )(o_f, o_b, p3, norm_g)


def _s5_prep_kernel(are_ref, aim_ref, ldt_ref, bre_ref, bim_ref, bbr_ref, bbi_ref, pwr_ref, pwi_ref):
    a_re = are_ref[...]
    a_im = aim_ref[...]
    dt = jnp.exp(ldt_ref[...])
    mag = jnp.exp(dt * a_re)
    ang = dt * a_im
    ab_re = mag * jnp.cos(ang)
    ab_im = mag * jnp.sin(ang)
    den = a_re * a_re + a_im * a_im
    x_ = ab_re - 1.0
    y_ = ab_im
    g_re = (x_ * a_re + y_ * a_im) / den
    g_im = (y_ * a_re - x_ * a_im) / den
    for c in range(S5_GROUP_CH):
        b_re = bre_ref[c]
        b_im = bim_ref[c]
        bbr_ref[c] = g_re * b_re - g_im * b_im
        bbi_ref[c] = g_re * b_im + g_im * b_re
    p_re = ab_re
    p_im = ab_im
    for i in range(S5_SUB):
        pwr_ref[i] = p_re
        pwi_ref[i] = p_im
        p_re, p_im = p_re * ab_re - p_im * ab_im, p_re * ab_im + p_im * ab_re


def _s5_prep(a_re, a_im, log_dt, b_re, b_im):
    n = 2 * S5_GROUPS
    sb = jax.ShapeDtypeStruct((S5_GROUP_CH, n, S5_STATE), F32)
    spw = jax.ShapeDtypeStruct((S5_SUB, n, S5_STATE), F32)
    return pl.pallas_call(
        _s5_prep_kernel,
        out_shape=[sb, sb, spw, spw],
        name="s5_prep",
    )(a_re.reshape(n, S5_STATE), a_im.reshape(n, S5_STATE), log_dt.reshape(n, 1),
      jnp.transpose(b_re.reshape(n, S5_STATE, S5_GROUP_CH), (2, 0, 1)),
      jnp.transpose(b_im.reshape(n, S5_STATE, S5_GROUP_CH), (2, 0, 1)))


def _s5_layout(bbr, bbi, pwr, pwi, c_re, c_im):
    eye = jnp.eye(S5_GPT, dtype=F32)

    def pw_tiles(x):
        return jnp.transpose(x.reshape(S5_SUB, 2, S5_LT, S5_SW), (1, 2, 0, 3))

    pw_t = jnp.stack([pw_tiles(pwr), pw_tiles(pwi)], axis=1)

    def b_bd(x):
        x = jnp.transpose(x, (1, 0, 2)).reshape(2, S5_LT, S5_GPT, S5_GROUP_CH, S5_STATE)
        bd = x[:, :, :, :, None, :] * eye[None, None, :, None, :, None]
        return bd.reshape(2, S5_LT, LANES, S5_SW)

    b_t = jnp.concatenate([b_bd(bbr), b_bd(bbi)], axis=-1)

    def c_bd(x):
        x = jnp.transpose(x.reshape(2, S5_LT, S5_GPT, S5_GROUP_CH, S5_STATE), (0, 1, 2, 4, 3))
        bd = x[:, :, :, :, None, :] * eye[None, None, :, None, :, None]
        return bd.reshape(2, S5_LT, S5_SW, LANES)

    pw_c = jnp.transpose(pw_t, (0, 1, 2, 4, 3))
    return pw_t, pw_c, b_t, c_bd(c_re), c_bd(-c_im)


def _s5_operators(prep, c_re, c_im):
    pw_t, pw_c, b_t, cre_t, cimn_t = _s5_layout(*prep, c_re, c_im)
    b8_t, kint_t, ca_t = _s5_block_prep(pw_t, pw_c, b_t, cre_t, cimn_t)
    return pw_t, b8_t, kint_t, ca_t


def _s5_perm():
    n = SUBLANES * SUBLANES
    r = np.arange(n)
    p = np.zeros((n, n), np.float32)
    p[r, (r % SUBLANES) * SUBLANES + r // SUBLANES] = 1.0
    return p


def _dot3(x, y):
    xh = x.astype(BF16)
    xl = (x - xh.astype(F32)).astype(BF16)
    yh = y.astype(BF16)
    yl = (y - yh.astype(F32)).astype(BF16)
    return (jnp.dot(xh, yh, preferred_element_type=F32) + jnp.dot(xh, yl, preferred_element_type=F32)
            + jnp.dot(xl, yh, preferred_element_type=F32))


def _s5_block_kernel(b_ref, cre_ref, cimn_ref, pw_ref, pwc_ref, b8_ref, kint_ref, ca_ref):
    n = S5_BLK
    for d in range(2):
        b = b_ref[d, 0]
        cre = cre_ref[d, 0]
        cimn = cimn_ref[d, 0]
        cfull = jnp.concatenate([cre, cimn], axis=0)

        def b_pow(k, d=d, b=b):
            if k == 0:
                return b
            pr = pw_ref[d, 0, 0, k - 1:k, :]
            pi = pw_ref[d, 1, 0, k - 1:k, :]
            bre, bim = b[:, :S5_SW], b[:, S5_SW:]
            return jnp.concatenate([bre * pr - bim * pi, bre * pi + bim * pr], axis=1)

        def c_pow(k, d=d, cre=cre, cimn=cimn):
            ar = pwc_ref[d, 0, 0, :, k - 1:k]
            ai = pwc_ref[d, 1, 0, :, k - 1:k]
            return jnp.concatenate([cre * ar + cimn * ai, cimn * ar - cre * ai], axis=0)

        bp = [b_pow(k) for k in range(n)]
        kk = [_dot3(bp[k], cfull).astype(BF16) for k in range(n)]
        zero = jnp.zeros((LANES, LANES), BF16)
        for s in range(n):
            after = n - 1 - s if d == 0 else s
            b8_ref[d, 0, s * LANES:(s + 1) * LANES, :] = bp[after].astype(BF16)
            upto = s + 1 if d == 0 else n - s
            ca_ref[d, 0, :, s * LANES:(s + 1) * LANES] = c_pow(upto).astype(BF16)
            for r in range(n):
                lag = r - s if d == 0 else s - r
                kint_ref[d, 0, s * LANES:(s + 1) * LANES, r * LANES:(r + 1) * LANES] = kk[lag] if lag >= 0 else zero


def _s5_block_prep(pw_t, pw_c, b_t, cre_t, cimn_t):
    w = S5_BLK * LANES
    out = jax.ShapeDtypeStruct((2, S5_LT, w, w), BF16)
    out_b8 = jax.ShapeDtypeStruct((2, S5_LT, w, 2 * S5_SW), BF16)
    out_ca = jax.ShapeDtypeStruct((2, S5_LT, 2 * S5_SW, w), BF16)
    return pl.pallas_call(
        _s5_block_kernel,
        grid=(S5_LT,),
        in_specs=[pl.BlockSpec((2, 1, LANES, 2 * S5_SW), lambda l: (0, l, 0, 0)),
                  pl.BlockSpec((2, 1, S5_SW, LANES), lambda l: (0, l, 0, 0)),
                  pl.BlockSpec((2, 1, S5_SW, LANES), lambda l: (0, l, 0, 0)),
                  pl.BlockSpec((2, 2, 1, S5_SUB, S5_SW), lambda l: (0, 0, l, 0, 0)),
                  pl.BlockSpec((2, 2, 1, S5_SW, S5_SUB), lambda l: (0, 0, l, 0, 0))],
        out_specs=[pl.BlockSpec((2, 1, w, 2 * S5_SW), lambda l: (0, l, 0, 0)),
                   pl.BlockSpec((2, 1, w, w), lambda l: (0, l, 0, 0)),
                   pl.BlockSpec((2, 1, 2 * S5_SW, w), lambda l: (0, l, 0, 0))],
        out_shape=[out_b8, out, out_ca],
        compiler_params=_cparams(("parallel",)),
        name="s5_block_prep",
    )(b_t, cre_t, cimn_t, pw_t, pw_c)


def _vreg_grid_t(x):
    g = SUBLANES
    return jnp.concatenate(
        [jnp.concatenate([x[q * g:(q + 1) * g, p * LANES:(p + 1) * LANES] for q in range(g)], axis=1)
         for p in range(g)], axis=0)


def _s5_to_wide(x):
    g = SUBLANES
    return jnp.concatenate(
        [jnp.concatenate([x[(p * g + q) * g:(p * g + q + 1) * g, :] for q in range(g)], axis=1) for p in range(g)],
        axis=0)


def _s5_from_wide(x):
    g = SUBLANES
    return jnp.concatenate([x[p * g:(p + 1) * g, q * LANES:(q + 1) * LANES] for p in range(g) for q in range(g)],
                           axis=0)


def _s5_kernel(u_ref, pw_ref, b8_ref, kint_ref, ca_ref, d_ref, pm_ref, o_ref,
               drv_scr, y8_scr, hin_scr, *, seq_len):
    n_g = seq_len // (S5_TILE * S5_GROUP)
    nb = S5_SUB // S5_BLK
    rpt = SUBLANES * nb
    w = SUBLANES * LANES
    o_ref[0] = u_ref[0] * d_ref[...]
    zero_row = jnp.zeros((1, S5_SW), F32)
    zero_blk = jnp.zeros((SUBLANES, S5_SW), F32)

    def power(d, k):
        return pw_ref[d, 0, 0, k - 1:k, :], pw_ref[d, 1, 0, k - 1:k, :]

    def tile_rows(kg, i, d):
        t = kg * S5_GROUP + i
        kt = t if d == 0 else n_g * S5_GROUP - 1 - t
        return pl.ds(pl.multiple_of(kt * S5_TILE, S5_TILE), S5_TILE)

    def scan_tile(d, i, carry_re, carry_im):
        order = list(range(nb)) if d == 0 else list(range(nb - 1, -1, -1))
        ar, ai = (jnp.broadcast_to(x, (SUBLANES, S5_SW)) for x in power(d, S5_BLK))
        hr, hi = zero_blk, zero_blk
        loc = {}
        for m in order:
            rows = slice(i * rpt + m * SUBLANES, i * rpt + (m + 1) * SUBLANES)
            hr, hi = (ar * hr - ai * hi + drv_scr[d, rows, :S5_SW], ar * hi + ai * hr + drv_scr[d, rows, S5_SW:])
            loc[m] = (hr, hi)
        asr, asi = power(d, S5_SUB)
        c_re, c_im = carry_re, carry_im
        ent_re, ent_im = [None] * SUBLANES, [None] * SUBLANES
        for j in (range(SUBLANES) if d == 0 else range(SUBLANES - 1, -1, -1)):
            ent_re[j], ent_im[j] = c_re, c_im
            er, ei = hr[j:j + 1, :], hi[j:j + 1, :]
            c_re, c_im = er + asr * c_re - asi * c_im, ei + asr * c_im + asi * c_re
        ent_re = jnp.concatenate(ent_re, axis=0)
        ent_im = jnp.concatenate(ent_im, axis=0)
        prev_re, prev_im = {order[0]: ent_re}, {order[0]: ent_im}
        for idx, m in enumerate(order[:-1]):
            pr, pi = (jnp.broadcast_to(x, (SUBLANES, S5_SW)) for x in power(d, S5_BLK * (idx + 1)))
            prev_re[order[idx + 1]] = loc[m][0] + pr * ent_re - pi * ent_im
            prev_im[order[idx + 1]] = loc[m][1] + pr * ent_im + pi * ent_re
        h_in = jnp.concatenate([jnp.concatenate([prev_re[m] for m in range(nb)], axis=0),
                                jnp.concatenate([prev_im[m] for m in range(nb)], axis=0)], axis=1)
        hin_scr[d, i * rpt:(i + 1) * rpt, :] = h_in.astype(BF16)
        return c_re, c_im

    def group_body(kg, carry):
        for d in range(2):
            wide = jnp.concatenate([_s5_to_wide(u_ref[0, tile_rows(kg, i, d), :]) for i in range(S5_GROUP)], axis=1)
            perm = jnp.dot(pm_ref[...], wide.astype(BF16), preferred_element_type=F32)
            u8 = jnp.concatenate([_vreg_grid_t(perm[:, i * w:(i + 1) * w]) for i in range(S5_GROUP)],
                                 axis=0).astype(BF16)
            drv_scr[d] = jnp.dot(u8, b8_ref[d, 0], preferred_element_type=F32)
            y8_scr[d] = jnp.dot(u8, kint_ref[d, 0], preferred_element_type=F32)
        new_carry = []
        for d in range(2):
            c_re, c_im = carry[2 * d], carry[2 * d + 1]
            for i in range(S5_GROUP):
                c_re, c_im = scan_tile(d, i, c_re, c_im)
            new_carry += [c_re, c_im]
        for d in range(2):
            y = y8_scr[d] + jnp.dot(hin_scr[d], ca_ref[d, 0], preferred_element_type=F32)
            yt = jnp.concatenate([_vreg_grid_t(y[i * rpt:(i + 1) * rpt]) for i in range(S5_GROUP)], axis=1)
            y_hi = yt.astype(BF16)
            y_lo = (yt - y_hi.astype(F32)).astype(BF16)
            yn = (jnp.dot(pm_ref[...], y_hi, preferred_element_type=F32)
                  + jnp.dot(pm_ref[...], y_lo, preferred_element_type=F32))
            for i in range(S5_GROUP):
                rows = tile_rows(kg, i, d)
                o_ref[0, rows, :] = o_ref[0, rows, :] + _s5_from_wide(yn[:, i * w:(i + 1) * w])
        return tuple(new_carry)

    lax.fori_loop(0, n_g, group_body, (zero_row,) * 4)


def _s5_scan(p3, pw_t, b8_t, kint_t, ca_t, d_skip, perm):
    bsz, seq_len, _ = p3.shape
    cpb = W_BRANCH // LANES
    w = S5_BLK * LANES
    rows = S5_GROUP * SUBLANES * (S5_SUB // S5_BLK)
    return pl.pallas_call(
        functools.partial(_s5_kernel, seq_len=seq_len),
        grid=(S5_LT, bsz),
        in_specs=[pl.BlockSpec((1, seq_len, LANES), lambda l, b: (b, 0, COL_D_IN * cpb + l)),
                  pl.BlockSpec((2, 2, 1, S5_SUB, S5_SW), lambda l, b: (0, 0, l, 0, 0)),
                  pl.BlockSpec((2, 1, w, 2 * S5_SW), lambda l, b: (0, l, 0, 0)),
                  pl.BlockSpec((2, 1, w, w), lambda l, b: (0, l, 0, 0)),
                  pl.BlockSpec((2, 1, 2 * S5_SW, w), lambda l, b: (0, l, 0, 0)),
                  pl.BlockSpec((1, LANES), lambda l, b: (0, l)),
                  pl.BlockSpec((SUBLANES * SUBLANES, SUBLANES * SUBLANES), lambda l, b: (0, 0))],
        out_specs=pl.BlockSpec((1, seq_len, LANES), lambda l, b: (b, 0, l)),
        out_shape=jax.ShapeDtypeStruct((bsz, seq_len, W_BRANCH), F32),
        scratch_shapes=[pltpu.VMEM((2, rows, 2 * S5_SW), F32),
                        pltpu.VMEM((2, rows, w), F32),
                        pltpu.VMEM((2, rows, 2 * S5_SW), BF16)],
        compiler_params=_cparams(("parallel", "parallel")),
        name="s5_scan",
    )(p3, pw_t, b8_t, kint_t, ca_t, d_skip, perm)


def _s5_post_kernel(y_ref, gate_ref, w_ref, b_ref, o_ref):
    y = y_ref[0]
    c0 = math.sqrt(2.0 / math.pi)
    z = 0.5 * y * (1.0 + jnp.tanh(c0 * (y + 0.044715 * (y * y * y))))
    lin = jnp.dot(z.astype(BF16), w_ref[...], preferred_element_type=F32) + b_ref[...]
    out = z * _sigmoid(lin) * _silu(gate_ref[0])
    o_ref[0] = out.astype(o_ref.dtype)


def _s5_post(y, p3, glu_w_bf, glu_b, tt=512):
    bsz, seq_len, _ = p3.shape
    blk = pl.BlockSpec((1, tt, W_BRANCH), lambda b, t: (b, t, 0))
    return pl.pallas_call(
        _s5_post_kernel,
        grid=(bsz, seq_len // tt),
        in_specs=[blk,
                  pl.BlockSpec((1, tt, W_BRANCH), lambda b, t: (b, t, COL_D_GATE)),
                  pl.BlockSpec((W_BRANCH, W_BRANCH), lambda b, t: (0, 0)),
                  pl.BlockSpec((1, W_BRANCH), lambda b, t: (0, 0))],
        out_specs=blk,
        out_shape=jax.ShapeDtypeStruct((bsz, seq_len, W_BRANCH), BF16),
        compiler_params=_cparams(("parallel", "parallel")),
        name="s5_post",
    )(y, p3, glu_w_bf, glu_b)


def _merge_kernel(ya_ref, yb_ref, yc_ref, yd_ref, r0_ref, r1_ref, r2_ref, r3_ref, w_ref, o_ref):
    acc = None
    for i, (y_ref, r_ref) in enumerate(((ya_ref, r0_ref), (yb_ref, r1_ref), (yc_ref, r2_ref), (yd_ref, r3_ref))):
        term = _sigmoid(r_ref[...].astype(F32)) * jnp.dot(y_ref[...], w_ref[i], preferred_element_type=F32)
        acc = term if acc is None else acc + term
    o_ref[...] = acc.astype(o_ref.dtype)


def _merge(ys, r2, w_stack_bf, tm=1024, tn=1024):
    t = r2.shape[0]
    npb = D_MODEL // tn
    yspec = pl.BlockSpec((tm, W_BRANCH), lambda j, i: (i, 0))

    def rspec(br):
        return pl.BlockSpec((tm, tn), lambda j, i: (i, br * npb + j))

    return pl.pallas_call(
        _merge_kernel,
        grid=(npb, t // tm),
        in_specs=[yspec, yspec, yspec, yspec, rspec(0), rspec(1), rspec(2), rspec(3),
                  pl.BlockSpec((N_BRANCH, W_BRANCH, tn), lambda j, i: (0, 0, j), pipeline_mode=pl.Buffered(1))],
        out_specs=pl.BlockSpec((tm, tn), lambda j, i: (i, j)),
        out_shape=jax.ShapeDtypeStruct((t, D_MODEL), BF16),
        compiler_params=_cparams(("parallel", "parallel")),
        name="merge",
    )(*ys, r2, r2, r2, r2, w_stack_bf)


def _out_proj_kernel(x_ref, m_ref, w_ref, g_ref, o_ref, *, final_norm):
    y = x_ref[...] + jnp.dot(m_ref[...], w_ref[...], preferred_element_type=F32)
    if final_norm:
        y = y * lax.rsqrt(jnp.mean(y * y, axis=-1, keepdims=True) + EPS) * g_ref[...]
    o_ref[...] = y


def _out_proj(x2, m, w_bf, final_g, final_norm, tm=512):
    t = x2.shape[0]
    blk = pl.BlockSpec((tm, D_MODEL), lambda i: (i, 0))
    return pl.pallas_call(
        functools.partial(_out_proj_kernel, final_norm=final_norm),
        grid=(t // tm,),
        in_specs=[blk, blk,
                  pl.BlockSpec((D_MODEL, D_MODEL), lambda i: (0, 0)),
                  pl.BlockSpec((1, D_MODEL), lambda i: (0, 0))],
        out_specs=blk,
        out_shape=jax.ShapeDtypeStruct((t, D_MODEL), F32),
        compiler_params=_cparams(("parallel",)),
        name="out_proj",
    )(x2, m, w_bf, final_g)


def _lb_kernel(x_ref, o_ref):
    x = x_ref[...]
    e = jnp.exp(x - jnp.max(x, axis=0, keepdims=True))
    sm = e / jnp.sum(e, axis=0, keepdims=True)
    run = jnp.zeros_like(sm[0])
    for l in range(DEPTH):
        run = run + sm[l]
        o_ref[l] = run - sm[0]


def _lower_bounds(hg_lb):
    return pl.pallas_call(
        _lb_kernel,
        out_shape=jax.ShapeDtypeStruct(hg_lb.shape, F32),
        name="hg_lower_bounds",
    )(hg_lb)


def _layer(x, lw, final_g, final_norm):
    bsz, seq_len, _ = x.shape
    x2 = x.reshape(bsz * seq_len, D_MODEL)
    p2, r2 = _in_proj(x2, lw["norm_g"], lw["w_in"])
    p3 = p2.reshape(bsz, seq_len, COL_R * W_BRANCH)
    ya = _conv_branch(p3, lw["conv_w"], lw["conv_b"], lw["conv_ln_g"], lw["conv_ln_b"])
    yb = _pool_branch(p3, lw["pool_w"], lw["pool_scale"])
    o_f, o_b = _hg_scan(p3, lw["lb"], lw["hg_nm"], lw["hg_mk"])
    yc = _hg_post(o_f, o_b, p3, lw["hg_norm_g"])
    y5 = _s5_scan(p3, *lw["s5"], lw["s5_d"], lw["s5_perm"])
    yd = _s5_post(y5, p3, lw["s5_glu_w"], lw["s5_glu_b"])
    ys = [y.reshape(bsz * seq_len, W_BRANCH) for y in (ya, yb, yc, yd)]
    m = _merge(ys, r2, lw["w_branch_out"])
    out = _out_proj(x2, m, lw["w_out"], final_g, final_norm)
    return out.reshape(bsz, seq_len, D_MODEL)


def kernel(x_prompt, x_sample, norm_g, w_in, conv_w, conv_b, conv_ln_g, conv_ln_b, w_a_out, pool_w, pool_scale, w_b_out, hg_lb, hg_norm_g, w_c_out, s5_a_re, s5_a_im, s5_log_dt, s5_b_re, s5_b_im, s5_c_re, s5_c_im, s5_d, s5_glu_w, s5_glu_b, w_d_out, w_out, final_g):
    lb_all = _lower_bounds(hg_lb)
    nm, mk = _hg_tables()
    nm_bf = jnp.asarray(nm, BF16)
    mk_bf = jnp.asarray(mk, BF16)
    perm_bf = jnp.asarray(_s5_perm(), BF16)
    layers = []
    for l in range(DEPTH):
        prep = _s5_prep(s5_a_re[l], s5_a_im[l], s5_log_dt[l], s5_b_re[l], s5_b_im[l])
        layers.append(dict(
            norm_g=norm_g[l][None], w_in=w_in[l].astype(BF16),
            conv_w=conv_w[l], conv_b=conv_b[l][None], conv_ln_g=conv_ln_g[l][None], conv_ln_b=conv_ln_b[l][None],
            pool_w=pool_w[l].astype(BF16), pool_scale=pool_scale[l][None],
            lb=lb_all[l], hg_nm=nm_bf, hg_mk=mk_bf, hg_norm_g=hg_norm_g[l][None],
            s5=_s5_operators(prep, s5_c_re[l], s5_c_im[l]), s5_d=s5_d[l][None],
            s5_perm=perm_bf,
            s5_glu_w=s5_glu_w[l].astype(BF16), s5_glu_b=s5_glu_b[l][None],
            w_branch_out=jnp.stack([w_a_out[l], w_b_out[l], w_c_out[l], w_d_out[l]]).astype(BF16),
            w_out=w_out[l].astype(BF16)))
    fg = final_g[None]

    def run(x):
        for l in range(DEPTH):
            x = _layer(x, layers[l], fg, l == DEPTH - 1)
        return x

    return (run(x_prompt), run(x_sample))
```

```python
import functools
import math

import numpy as np
import jax
import jax.numpy as jnp
from jax import lax
from jax.experimental import pallas as pl
from jax.experimental.pallas import tpu as pltpu

F32 = jnp.float32
BF16 = jnp.bfloat16

D_MODEL = 2048
DEPTH = 2
W_BRANCH = 1024
N_BRANCH = 4
N_IN = 12 * W_BRANCH + N_BRANCH * D_MODEL
CONV_WIDTH = 31
CONV_PAD = CONV_WIDTH // 2
POOL_WINDOWS = (2, 4, 8, 16)
POOL_GROUP = W_BRANCH // len(POOL_WINDOWS)
HG_HEAD_DIM = 128
HG_HEADS = W_BRANCH // HG_HEAD_DIM
S5_GROUP_CH = 16
S5_GROUPS = W_BRANCH // S5_GROUP_CH
S5_STATE = 64
EPS = 1e-6

LANES = 128
SUBLANES = 8
VMEM_LIMIT = 58 * 1024 * 1024

COL_A_VAL, COL_A_GLU, COL_A_GATE, COL_B_IN, COL_B_GATE = 0, 1, 2, 3, 4
COL_C_Q, COL_C_FF, COL_C_FB, COL_C_I, COL_C_GATE, COL_D_IN, COL_D_GATE = 5, 6, 7, 8, 9, 10, 11
COL_R = 12

HALO = 16
CONV_LANES = 256
HG_TILE = 256
HG_HALF = HG_TILE // 2
HG_HPS = 8
HG_LEVELS = 8
HG_NROWS = 4 * HG_TILE
S5_TILE = 512
S5_SUB = S5_TILE // SUBLANES
S5_BLK = 8
S5_GROUP = 4
S5_LT = W_BRANCH // LANES
S5_GPT = LANES // S5_GROUP_CH
S5_SW = S5_GPT * S5_STATE


def _sigmoid(x):
    return 1.0 / (1.0 + jnp.exp(-x))


def _silu(x):
    return x * _sigmoid(x)


def _cparams(sem):
    return pltpu.CompilerParams(dimension_semantics=sem, vmem_limit_bytes=VMEM_LIMIT)


def _in_proj_kernel(x_ref, g_ref, w_ref, o_ref, r_ref, h_ref, *, n_main):
    j = pl.program_id(1)

    @pl.when(j == 0)
    def _():
        x = x_ref[...]
        ms = jnp.mean(x * x, axis=-1, keepdims=True)
        h_ref[...] = (x * lax.rsqrt(ms + EPS) * g_ref[...]).astype(BF16)

    @pl.when(j < n_main)
    def _():
        o_ref[...] = jnp.dot(h_ref[...], w_ref[...], preferred_element_type=F32)

    @pl.when(j >= n_main)
    def _():
        r_ref[...] = jnp.dot(h_ref[...], w_ref[...], preferred_element_type=F32).astype(BF16)


def _in_proj(x2, g, w_bf, tm=1024, tn=2048):
    t = x2.shape[0]
    n_main = COL_R * W_BRANCH // tn
    return pl.pallas_call(
        functools.partial(_in_proj_kernel, n_main=n_main),
        grid=(t // tm, N_IN // tn),
        in_specs=[pl.BlockSpec((tm, D_MODEL), lambda i, j: (i, 0), pipeline_mode=pl.Buffered(1)),
                  pl.BlockSpec((1, D_MODEL), lambda i, j: (0, 0)),
                  pl.BlockSpec((D_MODEL, tn), lambda i, j: (0, j))],
        out_specs=[pl.BlockSpec((tm, tn), lambda i, j: (i, jnp.minimum(j, n_main - 1))),
                   pl.BlockSpec((tm, tn), lambda i, j: (i, jnp.maximum(j - n_main, 0)))],
        out_shape=[jax.ShapeDtypeStruct((t, COL_R * W_BRANCH), F32),
                   jax.ShapeDtypeStruct((t, N_BRANCH * D_MODEL), BF16)],
        scratch_shapes=[pltpu.VMEM((tm, D_MODEL), BF16)],
        compiler_params=_cparams(("parallel", "arbitrary")),
        name="in_proj",
    )(x2, g, w_bf)


def _conv_kernel(main_ref, gate_ref, prev_ref, next_ref, cw_ref, cb_ref, lg_ref, lb_ref, o_ref,
                 us_scr, wb_scr, *, tt, rc):
    t = pl.program_id(1)
    n_t = pl.num_programs(1)

    def glu(ref):
        return ref[0, :, :W_BRANCH] * _sigmoid(ref[0, :, W_BRANCH:])

    us_scr[0, HALO:HALO + tt, :] = glu(main_ref)
    us_scr[0, 0:HALO, :] = jnp.where(t > 0, glu(prev_ref), 0.0)
    us_scr[0, HALO + tt:2 * HALO + tt, :] = jnp.where(t < n_t - 1, glu(next_ref), 0.0)
    n = tt + 2 * HALO - SUBLANES
    for s in range(1, SUBLANES):
        us_scr[s, 0:n, :] = us_scr[0, s:s + n, :]
    for j in range(CONV_WIDTH):
        wb_scr[j] = jnp.broadcast_to(cw_ref[j:j + 1, :], (SUBLANES, W_BRANCH))

    def body(c, carry):
        r0 = pl.multiple_of(c * rc, rc)
        parts = []
        for lc in range(W_BRANCH // CONV_LANES):
            lanes = slice(lc * CONV_LANES, (lc + 1) * CONV_LANES)
            acc = jnp.zeros((rc // SUBLANES, SUBLANES, CONV_LANES), F32)
            for j in range(CONV_WIDTH):
                off = HALO - CONV_PAD + j
                s = off % SUBLANES
                tap = us_scr[s, pl.ds(r0 + (off - s), rc), lanes]
                acc = acc + tap.reshape(rc // SUBLANES, SUBLANES, CONV_LANES) * wb_scr[j, :, lanes]
            parts.append(acc.reshape(rc, CONV_LANES))
        y = jnp.concatenate(parts, axis=-1) + cb_ref[...]
        mu = jnp.mean(y, axis=-1, keepdims=True)
        yc = y - mu
        var = jnp.mean(yc * yc, axis=-1, keepdims=True)
        yn = yc * lax.rsqrt(var + EPS) * lg_ref[...] + lb_ref[...]
        out = _silu(yn) * _silu(gate_ref[0, pl.ds(r0, rc), :])
        o_ref[0, pl.ds(r0, rc), :] = out.astype(o_ref.dtype)
        return carry

    lax.fori_loop(0, tt // rc, body, 0)


def _halo_specs(tt, width, col_blk, seq_len):
    per = tt // HALO
    last = seq_len // HALO - 1
    prev = pl.BlockSpec((1, HALO, width), lambda b, t: (b, jnp.maximum(t * per - 1, 0), col_blk))
    nxt = pl.BlockSpec((1, HALO, width), lambda b, t: (b, jnp.minimum((t + 1) * per, last), col_blk))
    return prev, nxt


def _conv_branch(p3, cw, cb, lg, lb, tt=512, rc=64):
    bsz, seq_len, _ = p3.shape
    prev, nxt = _halo_specs(tt, 2 * W_BRANCH, 0, seq_len)
    vec = pl.BlockSpec((1, W_BRANCH), lambda b, t: (0, 0))
    return pl.pallas_call(
        functools.partial(_conv_kernel, tt=tt, rc=rc),
        grid=(bsz, seq_len // tt),
        in_specs=[pl.BlockSpec((1, tt, 2 * W_BRANCH), lambda b, t: (b, t, 0)),
                  pl.BlockSpec((1, tt, W_BRANCH), lambda b, t: (b, t, COL_A_GATE)),
                  prev, nxt,
                  pl.BlockSpec((CONV_WIDTH, W_BRANCH), lambda b, t: (0, 0)),
                  vec, vec, vec],
        out_specs=pl.BlockSpec((1, tt, W_BRANCH), lambda b, t: (b, t, 0)),
        out_shape=jax.ShapeDtypeStruct((bsz, seq_len, W_BRANCH), BF16),
        scratch_shapes=[pltpu.VMEM((SUBLANES, tt + 2 * HALO, W_BRANCH), F32),
                        pltpu.VMEM((CONV_WIDTH, SUBLANES, W_BRANCH), F32)],
        compiler_params=_cparams(("parallel", "parallel")),
        name="conv_branch",
    )(p3, p3, p3, p3, cw, cb, lg, lb)


def _pool_kernel(main_ref, gate_ref, prev_ref, next_ref, pw_ref, ps_ref, o_ref, e_scr, s_scr,
                 *, tt, seq_len):
    t = pl.program_id(1)
    n_t = pl.num_programs(1)
    rows = tt + 2 * HALO
    e_scr[HALO:HALO + tt, :] = main_ref[0]
    e_scr[0:HALO, :] = jnp.where(t > 0, prev_ref[0], 0.0)
    e_scr[HALO + tt:rows, :] = jnp.where(t < n_t - 1, next_ref[0], 0.0)

    tpos = t * tt + lax.broadcasted_iota(jnp.int32, (tt, 1), 0)
    for g, win in enumerate(POOL_WINDOWS):
        lanes = slice(g * POOL_GROUP, (g + 1) * POOL_GROUP)
        n = rows - 2
        s_scr[1:1 + n, lanes] = e_scr[1:1 + n, lanes] + e_scr[0:n, lanes]
        half = 1
        while 2 * half < win:
            n = rows - 2 * half - 2 * half
            lo = 2 * half
            a = s_scr[lo + half:lo + half + n, lanes]
            b = s_scr[lo - half:lo - half + n, lanes]
            s_scr[lo:lo + n, lanes] = a + b
            half *= 2
        left = win // 2
        right = win - 1 - left
        cnt = (jnp.minimum(tpos + right, seq_len - 1) + 1 - jnp.maximum(tpos - left, 0)).astype(F32)
        u = e_scr[HALO:HALO + tt, lanes]
        d = s_scr[HALO:HALO + tt, lanes] / cnt - u
        y = jnp.dot(d.astype(BF16), pw_ref[g], preferred_element_type=F32)
        y = y * ps_ref[:, lanes] * _silu(gate_ref[0, :, lanes])
        o_ref[0, :, lanes] = y.astype(o_ref.dtype)


def _pool_branch(p3, pw_bf, ps, tt=512):
    bsz, seq_len, _ = p3.shape
    prev, nxt = _halo_specs(tt, W_BRANCH, COL_B_IN, seq_len)
    return pl.pallas_call(
        functools.partial(_pool_kernel, tt=tt, seq_len=seq_len),
        grid=(bsz, seq_len // tt),
        in_specs=[pl.BlockSpec((1, tt, W_BRANCH), lambda b, t: (b, t, COL_B_IN)),
                  pl.BlockSpec((1, tt, W_BRANCH), lambda b, t: (b, t, COL_B_GATE)),
                  prev, nxt,
                  pl.BlockSpec((len(POOL_WINDOWS), POOL_GROUP, POOL_GROUP), lambda b, t: (0, 0, 0)),
                  pl.BlockSpec((1, W_BRANCH), lambda b, t: (0, 0))],
        out_specs=pl.BlockSpec((1, tt, W_BRANCH), lambda b, t: (b, t, 0)),
        out_shape=jax.ShapeDtypeStruct((bsz, seq_len, W_BRANCH), BF16),
        scratch_shapes=[pltpu.VMEM((tt + 2 * HALO, W_BRANCH), F32),
                        pltpu.VMEM((tt + 2 * HALO, W_BRANCH), F32)],
        compiler_params=_cparams(("parallel", "parallel")),
        name="pool_branch",
    )(p3, p3, p3, p3, pw_bf, ps)


def _hg_tables():
    c = HG_TILE
    t = np.arange(c)
    rr = np.arange(c)[None, :]
    vstart = (t // SUBLANES) * SUBLANES
    pref = [t, vstart + 7, vstart + 3, vstart + np.where(t % SUBLANES < 4, 1, 5)]
    nmat = np.stack([rr <= p[:, None] for p in pref]).astype(np.float32)
    masks = np.zeros((HG_LEVELS, HG_HALF, HG_HALF), np.float32)
    masks[0] = np.eye(HG_HALF)
    th = np.arange(HG_HALF)
    for lvl in range(HG_LEVELS - 1):
        half = 1 << lvl
        bh = th // (2 * half)
        lh = (th % (2 * half)) >= half
        masks[1 + lvl] = (bh[:, None] == bh[None, :]) & lh[:, None] & (~lh)[None, :]
    nm = np.stack([nmat, nmat[:, ::-1, ::-1]]).reshape(2, HG_NROWS, c)
    mk = np.stack([masks, masks[:, ::-1, ::-1]])
    return nm, mk


def _dot_nt(a, b):
    return lax.dot_general(a, b, (((1,), (1,)), ((), ())), preferred_element_type=F32)


def _hg_ref_rows(lvl, d):
    out = []
    for m in range(HG_TILE // SUBLANES):
        block = 2 << lvl
        mid = (m * SUBLANES // block) * block + block // 2
        out.append(mid // SUBLANES - 1 if d == 0 else mid // SUBLANES)
    return out


def _hg_gates(q, z, v, lbv, nm_ref, d):
    c = HG_TILE
    sg = _sigmoid(z)
    f = lbv + (1.0 - lbv) * sg
    lf = jnp.log2(f)
    kk = (1.0 - lbv) * (1.0 - sg)
    hi = lf.astype(BF16)
    lo = (lf - hi.astype(F32)).astype(BF16)
    tab = jnp.dot(nm_ref[d], jnp.concatenate([hi, lo], axis=1), preferred_element_type=F32)
    tab = tab[:, :HG_HEAD_DIM] + tab[:, HG_HEAD_DIM:]
    b, b_grp, b_l2, b_l1 = (tab[i * c:(i + 1) * c] for i in range(4))
    b_end = b_grp[c - 1:c] if d == 0 else b_grp[0:1]
    n_g = c // SUBLANES
    sub = lax.broadcasted_iota(jnp.int32, (1, SUBLANES, HG_HEAD_DIM), 1)
    q3 = q.reshape(n_g, SUBLANES, HG_HEAD_DIM)
    k3 = kk.reshape(n_g, SUBLANES, HG_HEAD_DIM)
    xs = []
    for lvl in range(HG_LEVELS):
        if (1 << lvl) < SUBLANES:
            later = (jnp.right_shift(sub, lvl) & 1) == (1 - d)
            base = jnp.where(later, q3, k3).reshape(c, HG_HEAD_DIM)
        else:
            pick = [((m * SUBLANES) >> lvl) & 1 == (1 - d) for m in range(n_g)]
            base = jnp.concatenate([(q3 if p else k3)[m] for m, p in enumerate(pick)], axis=0)
        if lvl == 0:
            later0 = jnp.broadcast_to(later, (n_g, SUBLANES, HG_HEAD_DIM)).reshape(c, HG_HEAD_DIM)
            xs.append(jnp.where(later0, base * f, base))
            continue
        if lvl == 1:
            ref = b_l1
        elif lvl == 2:
            ref = b_l2
        else:
            ref = jnp.concatenate([b_grp[g * SUBLANES:(g + 1) * SUBLANES] for g in _hg_ref_rows(lvl, d)], axis=0)
        neg_abs = pltpu.bitcast(pltpu.bitcast(b - ref, jnp.uint32) | jnp.uint32(0x80000000), F32)
        xs.append(base * jnp.exp2(neg_abs))
    return dict(q=q, kk=kk, v=v, xs=xs, q_in=q * jnp.exp2(b), k_out=kk * jnp.exp2(b_end - b),
                bend=jnp.exp2(b_end))


def _hg_kernel(qf_ref, zf_ref, vf_ref, qb_ref, zb_ref, vb_ref, lb_ref, nm_ref, mk_ref,
               of_ref, ob_ref, sf_scr, sb_scr):
    @pl.when(pl.program_id(2) == 0)
    def _():
        sf_scr[...] = jnp.zeros_like(sf_scr)
        sb_scr[...] = jnp.zeros_like(sb_scr)

    for hh in range(HG_HPS):
        _hg_head(hh, (qf_ref, zf_ref, vf_ref), (qb_ref, zb_ref, vb_ref), lb_ref, nm_ref, mk_ref,
                 (of_ref, ob_ref), (sf_scr, sb_scr))


def _hg_head(hh, ins_f, ins_b, lb_ref, nm_ref, mk_ref, o_refs, st_refs):
    c = HG_TILE
    halves = (slice(0, HG_HALF), slice(HG_HALF, c))
    lanes = slice(hh * HG_HEAD_DIM, (hh + 1) * HG_HEAD_DIM)
    ins = (ins_f, ins_b)
    g = [_hg_gates(ins[d][0][0, :, lanes], ins[d][1][0, :, lanes], ins[d][2][0, :, lanes],
                   lb_ref[d:d + 1, lanes], nm_ref, d) for d in range(2)]
    scores = [[[], []] for _ in range(2)]
    for d in range(2):
        qb = g[d]["q"].astype(BF16)
        kb = g[d]["kk"].astype(BF16)
        for i, rs in enumerate(halves):
            scores[d][i].append((_dot_nt(qb[rs], kb[rs]), 0))
        for lvl in range(HG_LEVELS - 1):
            x = g[d]["xs"][lvl].astype(BF16)
            for i, rs in enumerate(halves):
                scores[d][i].append((_dot_nt(x[rs], x[rs]), 1 + lvl))
        x = g[d]["xs"][HG_LEVELS - 1].astype(BF16)
        late, early = (1, 0) if d == 0 else (0, 1)
        scores[d][late].append((_dot_nt(x[halves[late]], x[halves[early]]), None))
    outs = []
    for d in range(2):
        vb = g[d]["v"].astype(BF16)
        late, early = (1, 0) if d == 0 else (0, 1)
        acc = []
        for i, rs in enumerate(halves):
            sm = [s.astype(BF16) * mk_ref[d, m] if m is not None else s.astype(BF16) for s, m in scores[d][i]]
            vs = [vb[rs]] * HG_LEVELS + ([vb[halves[early]]] if i == late else [])
            acc.append(jnp.dot(jnp.concatenate(sm, axis=1), jnp.concatenate(vs, axis=0),
                               preferred_element_type=F32))
        outs.append(acc)
    for d in range(2):
        st = st_refs[d][hh]
        o_refs[d][0, :, lanes] = (jnp.concatenate(outs[d], axis=0)
                                  + _dot_nt(g[d]["q_in"].astype(BF16), st.astype(BF16))).astype(o_refs[d].dtype)
        vt = jnp.transpose(g[d]["v"]).astype(BF16)
        st_refs[d][hh] = st * g[d]["bend"] + jnp.dot(vt, g[d]["k_out"].astype(BF16),
                                                     preferred_element_type=F32)


def _hg_scan(p3, lb, nm_bf, mk_bf):
    bsz, seq_len, _ = p3.shape
    tt = HG_TILE
    n_t = seq_len // tt
    hw = HG_HPS * HG_HEAD_DIM
    cpb = W_BRANCH // hw

    def fspec(col):
        return pl.BlockSpec((1, tt, hw), lambda b, h, k: (b, k, col * cpb + h))

    def bspec(col):
        return pl.BlockSpec((1, tt, hw), lambda b, h, k: (b, n_t - 1 - k, col * cpb + h))

    out = jax.ShapeDtypeStruct((bsz, seq_len, W_BRANCH), BF16)
    return pl.pallas_call(
        _hg_kernel,
        grid=(bsz, HG_HEADS // HG_HPS, n_t),
        in_specs=[fspec(COL_C_Q), fspec(COL_C_FF), fspec(COL_C_I),
                  bspec(COL_C_Q), bspec(COL_C_FB), bspec(COL_C_I),
                  pl.BlockSpec((2, hw), lambda b, h, k: (0, h)),
                  pl.BlockSpec((2, HG_NROWS, HG_TILE), lambda b, h, k: (0, 0, 0)),
                  pl.BlockSpec((2, HG_LEVELS, HG_HALF, HG_HALF), lambda b, h, k: (0, 0, 0, 0))],
        out_specs=[pl.BlockSpec((1, tt, hw), lambda b, h, k: (b, k, h)),
                   pl.BlockSpec((1, tt, hw), lambda b, h, k: (b, n_t - 1 - k, h))],
        out_shape=[out, out],
        scratch_shapes=[pltpu.VMEM((HG_HPS, HG_HEAD_DIM, HG_HEAD_DIM), F32),
                        pltpu.VMEM((HG_HPS, HG_HEAD_DIM, HG_HEAD_DIM), F32)],
        compiler_params=_cparams(("parallel", "parallel", "arbitrary")),
        name="hgrn2_scan",
    )(p3, p3, p3, p3, p3, p3, lb, nm_bf, mk_bf)


def _hg_post_kernel(of_ref, ob_ref, gate_ref, g_ref, o_ref):
    for h in range(HG_HEADS):
        lanes = slice(h * HG_HEAD_DIM, (h + 1) * HG_HEAD_DIM)
        o = of_ref[0, :, lanes].astype(F32) + ob_ref[0, :, lanes].astype(F32)
        o = o * lax.rsqrt(jnp.mean(o * o, axis=-1, keepdims=True) + EPS)
        y = o * g_ref[:, lanes] * _silu(gate_ref[0, :, lanes])
        o_ref[0, :, lanes] = y.astype(o_ref.dtype)


def _hg_post(o_f, o_b, p3, norm_g, tt=512):
    bsz, seq_len, _ = p3.shape
    blk = pl.BlockSpec((1, tt, W_BRANCH), lambda b, t: (b, t, 0))
    return pl.pallas_call(
        _hg_post_kernel,
        grid=(bsz, seq_len // tt),
        in_specs=[blk, blk,
                  pl.BlockSpec((1, tt, W_BRANCH), lambda b, t: (b, t, COL_C_GATE)),
                  pl.BlockSpec((1, W_BRANCH), lambda b, t: (0, 0))],
        out_specs=blk,
        out_shape=jax.ShapeDtypeStruct((bsz, seq_len, W_BRANCH), BF16),
        compiler_params=_cparams(("parallel", "parallel")),
        name="hgrn2_post",
    )(o_f, o_b, p3, norm_g)


def _s5_prep_kernel(are_ref, aim_ref, ldt_ref, bre_ref, bim_ref, bbr_ref, bbi_ref, pwr_ref, pwi_ref):
    a_re = are_ref[...]
    a_im = aim_ref[...]
    dt = jnp.exp(ldt_ref[...])
    mag = jnp.exp(dt * a_re)
    ang = dt * a_im
    ab_re = mag * jnp.cos(ang)
    ab_im = mag * jnp.sin(ang)
    den = a_re * a_re + a_im * a_im
    x_ = ab_re - 1.0
    y_ = ab_im
    g_re = (x_ * a_re + y_ * a_im) / den
    g_im = (y_ * a_re - x_ * a_im) / den
    for c in range(S5_GROUP_CH):
        b_re = bre_ref[c]
        b_im = bim_ref[c]
        bbr_ref[c] = g_re * b_re - g_im * b_im
        bbi_ref[c] = g_re * b_im + g_im * b_re
    p_re = ab_re
    p_im = ab_im
    for i in range(S5_SUB):
        pwr_ref[i] = p_re
        pwi_ref[i] = p_im
        p_re, p_im = p_re * ab_re - p_im * ab_im, p_re * ab_im + p_im * ab_re


def _s5_prep(a_re, a_im, log_dt, b_re, b_im):
    n = 2 * S5_GROUPS
    sb = jax.ShapeDtypeStruct((S5_GROUP_CH, n, S5_STATE), F32)
    spw = jax.ShapeDtypeStruct((S5_SUB, n, S5_STATE), F32)
    return pl.pallas_call(
        _s5_prep_kernel,
        out_shape=[sb, sb, spw, spw],
        name="s5_prep",
    )(a_re.reshape(n, S5_STATE), a_im.reshape(n, S5_STATE), log_dt.reshape(n, 1),
      jnp.transpose(b_re.reshape(n, S5_STATE, S5_GROUP_CH), (2, 0, 1)),
      jnp.transpose(b_im.reshape(n, S5_STATE, S5_GROUP_CH), (2, 0, 1)))


def _s5_layout(bbr, bbi, pwr, pwi, c_re, c_im):
    eye = jnp.eye(S5_GPT, dtype=F32)

    def pw_tiles(x):
        return jnp.transpose(x.reshape(S5_SUB, 2, S5_LT, S5_SW), (1, 2, 0, 3))

    pw_t = jnp.stack([pw_tiles(pwr), pw_tiles(pwi)], axis=1)

    def b_bd(x):
        x = jnp.transpose(x, (1, 0, 2)).reshape(2, S5_LT, S5_GPT, S5_GROUP_CH, S5_STATE)
        bd = x[:, :, :, :, None, :] * eye[None, None, :, None, :, None]
        return bd.reshape(2, S5_LT, LANES, S5_SW)

    b_t = jnp.concatenate([b_bd(bbr), b_bd(bbi)], axis=-1)

    def c_bd(x):
        x = jnp.transpose(x.reshape(2, S5_LT, S5_GPT, S5_GROUP_CH, S5_STATE), (0, 1, 2, 4, 3))
        bd = x[:, :, :, :, None, :] * eye[None, None, :, None, :, None]
        return bd.reshape(2, S5_LT, S5_SW, LANES)

    pw_c = jnp.transpose(pw_t, (0, 1, 2, 4, 3))
    return pw_t, pw_c, b_t, c_bd(c_re), c_bd(-c_im)


def _s5_operators(prep, c_re, c_im):
    pw_t, pw_c, b_t, cre_t, cimn_t = _s5_layout(*prep, c_re, c_im)
    b8_t, kint_t, ca_t = _s5_block_prep(pw_t, pw_c, b_t, cre_t, cimn_t)
    return pw_t, b8_t, kint_t, ca_t


def _s5_perm():
    n = SUBLANES * SUBLANES
    r = np.arange(n)
    p = np.zeros((n, n), np.float32)
    p[r, (r % SUBLANES) * SUBLANES + r // SUBLANES] = 1.0
    return p


def _dot3(x, y):
    xh = x.astype(BF16)
    xl = (x - xh.astype(F32)).astype(BF16)
    yh = y.astype(BF16)
    yl = (y - yh.astype(F32)).astype(BF16)
    return (jnp.dot(xh, yh, preferred_element_type=F32) + jnp.dot(xh, yl, preferred_element_type=F32)
            + jnp.dot(xl, yh, preferred_element_type=F32))


def _s5_block_kernel(b_ref, cre_ref, cimn_ref, pw_ref, pwc_ref, b8_ref, kint_ref, ca_ref):
    n = S5_BLK
    for d in range(2):
        b = b_ref[d, 0]
        cre = cre_ref[d, 0]
        cimn = cimn_ref[d, 0]
        cfull = jnp.concatenate([cre, cimn], axis=0)

        def b_pow(k, d=d, b=b):
            if k == 0:
                return b
            pr = pw_ref[d, 0, 0, k - 1:k, :]
            pi = pw_ref[d, 1, 0, k - 1:k, :]
            bre, bim = b[:, :S5_SW], b[:, S5_SW:]
            return jnp.concatenate([bre * pr - bim * pi, bre * pi + bim * pr], axis=1)

        def c_pow(k, d=d, cre=cre, cimn=cimn):
            ar = pwc_ref[d, 0, 0, :, k - 1:k]
            ai = pwc_ref[d, 1, 0, :, k - 1:k]
            return jnp.concatenate([cre * ar + cimn * ai, cimn * ar - cre * ai], axis=0)

        bp = [b_pow(k) for k in range(n)]
        kk = [_dot3(bp[k], cfull).astype(BF16) for k in range(n)]
        zero = jnp.zeros((LANES, LANES), BF16)
        for s in range(n):
            after = n - 1 - s if d == 0 else s
            b8_ref[d, 0, s * LANES:(s + 1) * LANES, :] = bp[after].astype(BF16)
            upto = s + 1 if d == 0 else n - s
            ca_ref[d, 0, :, s * LANES:(s + 1) * LANES] = c_pow(upto).astype(BF16)
            for r in range(n):
                lag = r - s if d == 0 else s - r
                kint_ref[d, 0, s * LANES:(s + 1) * LANES, r * LANES:(r + 1) * LANES] = kk[lag] if lag >= 0 else zero


def _s5_block_prep(pw_t, pw_c, b_t, cre_t, cimn_t):
    w = S5_BLK * LANES
    out = jax.ShapeDtypeStruct((2, S5_LT, w, w), BF16)
    out_b8 = jax.ShapeDtypeStruct((2, S5_LT, w, 2 * S5_SW), BF16)
    out_ca = jax.ShapeDtypeStruct((2, S5_LT, 2 * S5_SW, w), BF16)
    return pl.pallas_call(
        _s5_block_kernel,
        grid=(S5_LT,),
        in_specs=[pl.BlockSpec((2, 1, LANES, 2 * S5_SW), lambda l: (0, l, 0, 0)),
                  pl.BlockSpec((2, 1, S5_SW, LANES), lambda l: (0, l, 0, 0)),
                  pl.BlockSpec((2, 1, S5_SW, LANES), lambda l: (0, l, 0, 0)),
                  pl.BlockSpec((2, 2, 1, S5_SUB, S5_SW), lambda l: (0, 0, l, 0, 0)),
                  pl.BlockSpec((2, 2, 1, S5_SW, S5_SUB), lambda l: (0, 0, l, 0, 0))],
        out_specs=[pl.BlockSpec((2, 1, w, 2 * S5_SW), lambda l: (0, l, 0, 0)),
                   pl.BlockSpec((2, 1, w, w), lambda l: (0, l, 0, 0)),
                   pl.BlockSpec((2, 1, 2 * S5_SW, w), lambda l: (0, l, 0, 0))],
        out_shape=[out_b8, out, out_ca],
        compiler_params=_cparams(("parallel",)),
        name="s5_block_prep",
    )(b_t, cre_t, cimn_t, pw_t, pw_c)


def _vreg_grid_t(x):
    g = SUBLANES
    return jnp.concatenate(
        [jnp.concatenate([x[q * g:(q + 1) * g, p * LANES:(p + 1) * LANES] for q in range(g)], axis=1)
         for p in range(g)], axis=0)


def _s5_to_wide(x):
    g = SUBLANES
    return jnp.concatenate(
        [jnp.concatenate([x[(p * g + q) * g:(p * g + q + 1) * g, :] for q in range(g)], axis=1) for p in range(g)],
        axis=0)


def _s5_from_wide(x):
    g = SUBLANES
    return jnp.concatenate([x[p * g:(p + 1) * g, q * LANES:(q + 1) * LANES] for p in range(g) for q in range(g)],
                           axis=0)


def _s5_kernel(u_ref, pw_ref, b8_ref, kint_ref, ca_ref, d_ref, pm_ref, o_ref,
               drv_scr, y8_scr, hin_scr, *, seq_len):
    n_g = seq_len // (S5_TILE * S5_GROUP)
    nb = S5_SUB // S5_BLK
    rpt = SUBLANES * nb
    w = SUBLANES * LANES
    o_ref[0] = u_ref[0] * d_ref[...]
    zero_row = jnp.zeros((1, S5_SW), F32)
    zero_blk = jnp.zeros((SUBLANES, S5_SW), F32)

    def power(d, k):
        return pw_ref[d, 0, 0, k - 1:k, :], pw_ref[d, 1, 0, k - 1:k, :]

    def tile_rows(kg, i, d):
        t = kg * S5_GROUP + i
        kt = t if d == 0 else n_g * S5_GROUP - 1 - t
        return pl.ds(pl.multiple_of(kt * S5_TILE, S5_TILE), S5_TILE)

    def scan_tile(d, i, carry_re, carry_im):
        order = list(range(nb)) if d == 0 else list(range(nb - 1, -1, -1))
        ar, ai = (jnp.broadcast_to(x, (SUBLANES, S5_SW)) for x in power(d, S5_BLK))
        hr, hi = zero_blk, zero_blk
        loc = {}
        for m in order:
            rows = slice(i * rpt + m * SUBLANES, i * rpt + (m + 1) * SUBLANES)
            hr, hi = (ar * hr - ai * hi + drv_scr[d, rows, :S5_SW], ar * hi + ai * hr + drv_scr[d, rows, S5_SW:])
            loc[m] = (hr, hi)
        asr, asi = power(d, S5_SUB)
        c_re, c_im = carry_re, carry_im
        ent_re, ent_im = [None] * SUBLANES, [None] * SUBLANES
        for j in (range(SUBLANES) if d == 0 else range(SUBLANES - 1, -1, -1)):
            ent_re[j], ent_im[j] = c_re, c_im
            er, ei = hr[j:j + 1, :], hi[j:j + 1, :]
            c_re, c_im = er + asr * c_re - asi * c_im, ei + asr * c_im + asi * c_re
        ent_re = jnp.concatenate(ent_re, axis=0)
        ent_im = jnp.concatenate(ent_im, axis=0)
        prev_re, prev_im = {order[0]: ent_re}, {order[0]: ent_im}
        for idx, m in enumerate(order[:-1]):
            pr, pi = (jnp.broadcast_to(x, (SUBLANES, S5_SW)) for x in power(d, S5_BLK * (idx + 1)))
            prev_re[order[idx + 1]] = loc[m][0] + pr * ent_re - pi * ent_im
            prev_im[order[idx + 1]] = loc[m][1] + pr * ent_im + pi * ent_re
        h_in = jnp.concatenate([jnp.concatenate([prev_re[m] for m in range(nb)], axis=0),
                                jnp.concatenate([prev_im[m] for m in range(nb)], axis=0)], axis=1)
        hin_scr[d, i * rpt:(i + 1) * rpt, :] = h_in.astype(BF16)
        return c_re, c_im

    def group_body(kg, carry):
        for d in range(2):
            wide = jnp.concatenate([_s5_to_wide(u_ref[0, tile_rows(kg, i, d), :]) for i in range(S5_GROUP)], axis=1)
            perm = jnp.dot(pm_ref[...], wide.astype(BF16), preferred_element_type=F32)
            u8 = jnp.concatenate([_vreg_grid_t(perm[:, i * w:(i + 1) * w]) for i in range(S5_GROUP)],
                                 axis=0).astype(BF16)
            drv_scr[d] = jnp.dot(u8, b8_ref[d, 0], preferred_element_type=F32)
            y8_scr[d] = jnp.dot(u8, kint_ref[d, 0], preferred_element_type=F32)
        new_carry = []
        for d in range(2):
            c_re, c_im = carry[2 * d], carry[2 * d + 1]
            for i in range(S5_GROUP):
                c_re, c_im = scan_tile(d, i, c_re, c_im)
            new_carry += [c_re, c_im]
        for d in range(2):
            y = y8_scr[d] + jnp.dot(hin_scr[d], ca_ref[d, 0], preferred_element_type=F32)
            yt = jnp.concatenate([_vreg_grid_t(y[i * rpt:(i + 1) * rpt]) for i in range(S5_GROUP)], axis=1)
            y_hi = yt.astype(BF16)
            y_lo = (yt - y_hi.astype(F32)).astype(BF16)
            yn = (jnp.dot(pm_ref[...], y_hi, preferred_element_type=F32)
                  + jnp.dot(pm_ref[...], y_lo, preferred_element_type=F32))
            for i in range(S5_GROUP):
                rows = tile_rows(kg, i, d)
                o_ref[0, rows, :] = o_ref[0, rows, :] + _s5_from_wide(yn[:, i * w:(i + 1) * w])
        return tuple(new_carry)

    lax.fori_loop(0, n_g, group_body, (zero_row,) * 4)


def _s5_scan(p3, pw_t, b8_t, kint_t, ca_t, d_skip, perm):
    bsz, seq_len, _ = p3.shape
    cpb = W_BRANCH // LANES
    w = S5_BLK * LANES
    rows = S5_GROUP * SUBLANES * (S5_SUB // S5_BLK)
    return pl.pallas_call(
        functools.partial(_s5_kernel, seq_len=seq_len),
        grid=(S5_LT, bsz),
        in_specs=[pl.BlockSpec((1, seq_len, LANES), lambda l, b: (b, 0, COL_D_IN * cpb + l)),
                  pl.BlockSpec((2, 2, 1, S5_SUB, S5_SW), lambda l, b: (0, 0, l, 0, 0)),
                  pl.BlockSpec((2, 1, w, 2 * S5_SW), lambda l, b: (0, l, 0, 0)),
                  pl.BlockSpec((2, 1, w, w), lambda l, b: (0, l, 0, 0)),
                  pl.BlockSpec((2, 1, 2 * S5_SW, w), lambda l, b: (0, l, 0, 0)),
                  pl.BlockSpec((1, LANES), lambda l, b: (0, l)),
                  pl.BlockSpec((SUBLANES * SUBLANES, SUBLANES * SUBLANES), lambda l, b: (0, 0))],
        out_specs=pl.BlockSpec((1, seq_len, LANES), lambda l, b: (b, 0, l)),
        out_shape=jax.ShapeDtypeStruct((bsz, seq_len, W_BRANCH), F32),
        scratch_shapes=[pltpu.VMEM((2, rows, 2 * S5_SW), F32),
                        pltpu.VMEM((2, rows, w), F32),
                        pltpu.VMEM((2, rows, 2 * S5_SW), BF16)],
        compiler_params=_cparams(("parallel", "parallel")),
        name="s5_scan",
    )(p3, pw_t, b8_t, kint_t, ca_t, d_skip, perm)


def _s5_post_kernel(y_ref, gate_ref, w_ref, b_ref, o_ref):
    y = y_ref[0]
    c0 = math.sqrt(2.0 / math.pi)
    z = 0.5 * y * (1.0 + jnp.tanh(c0 * (y + 0.044715 * (y * y * y))))
    lin = jnp.dot(z.astype(BF16), w_ref[...], preferred_element_type=F32) + b_ref[...]
    out = z * _sigmoid(lin) * _silu(gate_ref[0])
    o_ref[0] = out.astype(o_ref.dtype)


def _s5_post(y, p3, glu_w_bf, glu_b, tt=512):
    bsz, seq_len, _ = p3.shape
    blk = pl.BlockSpec((1, tt, W_BRANCH), lambda b, t: (b, t, 0))
    return pl.pallas_call(
        _s5_post_kernel,
        grid=(bsz, seq_len // tt),
        in_specs=[blk,
                  pl.BlockSpec((1, tt, W_BRANCH), lambda b, t: (b, t, COL_D_GATE)),
                  pl.BlockSpec((W_BRANCH, W_BRANCH), lambda b, t: (0, 0)),
                  pl.BlockSpec((1, W_BRANCH), lambda b, t: (0, 0))],
        out_specs=blk,
        out_shape=jax.ShapeDtypeStruct((bsz, seq_len, W_BRANCH), BF16),
        compiler_params=_cparams(("parallel", "parallel")),
        name="s5_post",
    )(y, p3, glu_w_bf, glu_b)


def _merge_kernel(ya_ref, yb_ref, yc_ref, yd_ref, r0_ref, r1_ref, r2_ref, r3_ref, w_ref, o_ref):
    acc = None
    for i, (y_ref, r_ref) in enumerate(((ya_ref, r0_ref), (yb_ref, r1_ref), (yc_ref, r2_ref), (yd_ref, r3_ref))):
        term = _sigmoid(r_ref[...].astype(F32)) * jnp.dot(y_ref[...], w_ref[i], preferred_element_type=F32)
        acc = term if acc is None else acc + term
    o_ref[...] = acc.astype(o_ref.dtype)


def _merge(ys, r2, w_stack_bf, tm=1024, tn=1024):
    t = r2.shape[0]
    npb = D_MODEL // tn
    yspec = pl.BlockSpec((tm, W_BRANCH), lambda j, i: (i, 0))

    def rspec(br):
        return pl.BlockSpec((tm, tn), lambda j, i: (i, br * npb + j))

    return pl.pallas_call(
        _merge_kernel,
        grid=(npb, t // tm),
        in_specs=[yspec, yspec, yspec, yspec, rspec(0), rspec(1), rspec(2), rspec(3),
                  pl.BlockSpec((N_BRANCH, W_BRANCH, tn), lambda j, i: (0, 0, j), pipeline_mode=pl.Buffered(1))],
        out_specs=pl.BlockSpec((tm, tn), lambda j, i: (i, j)),
        out_shape=jax.ShapeDtypeStruct((t, D_MODEL), BF16),
        compiler_params=_cparams(("parallel", "parallel")),
        name="merge",
    )(*ys, r2, r2, r2, r2, w_stack_bf)


def _out_proj_kernel(x_ref, m_ref, w_ref, g_ref, o_ref, *, final_norm):
    y = x_ref[...] + jnp.dot(m_ref[...], w_ref[...], preferred_element_type=F32)
    if final_norm:
        y = y * lax.rsqrt(jnp.mean(y * y, axis=-1, keepdims=True) + EPS) * g_ref[...]
    o_ref[...] = y


def _out_proj(x2, m, w_bf, final_g, final_norm, tm=512):
    t = x2.shape[0]
    blk = pl.BlockSpec((tm, D_MODEL), lambda i: (i, 0))
    return pl.pallas_call(
        functools.partial(_out_proj_kernel, final_norm=final_norm),
        grid=(t // tm,),
        in_specs=[blk, blk,
                  pl.BlockSpec((D_MODEL, D_MODEL), lambda i: (0, 0)),
                  pl.BlockSpec((1, D_MODEL), lambda i: (0, 0))],
        out_specs=blk,
        out_shape=jax.ShapeDtypeStruct((t, D_MODEL), F32),
        compiler_params=_cparams(("parallel",)),
        name="out_proj",
    )(x2, m, w_bf, final_g)


def _lb_kernel(x_ref, o_ref):
    x = x_ref[...]
    e = jnp.exp(x - jnp.max(x, axis=0, keepdims=True))
    sm = e / jnp.sum(e, axis=0, keepdims=True)
    run = jnp.zeros_like(sm[0])
    for l in range(DEPTH):
        run = run + sm[l]
        o_ref[l] = run - sm[0]


def _lower_bounds(hg_lb):
    return pl.pallas_call(
        _lb_kernel,
        out_shape=jax.ShapeDtypeStruct(hg_lb.shape, F32),
        name="hg_lower_bounds",
    )(hg_lb)


def _layer(x, lw, final_g, final_norm):
    bsz, seq_len, _ = x.shape
    x2 = x.reshape(bsz * seq_len, D_MODEL)
    p2, r2 = _in_proj(x2, lw["norm_g"], lw["w_in"])
    p3 = p2.reshape(bsz, seq_len, COL_R * W_BRANCH)
    ya = _conv_branch(p3, lw["conv_w"], lw["conv_b"], lw["conv_ln_g"], lw["conv_ln_b"])
    yb = _pool_branch(p3, lw["pool_w"], lw["pool_scale"])
    o_f, o_b = _hg_scan(p3, lw["lb"], lw["hg_nm"], lw["hg_mk"])
    yc = _hg_post(o_f, o_b, p3, lw["hg_norm_g"])
    y5 = _s5_scan(p3, *lw["s5"], lw["s5_d"], lw["s5_perm"])
    yd = _s5_post(y5, p3, lw["s5_glu_w"], lw["s5_glu_b"])
    ys = [y.reshape(bsz * seq_len, W_BRANCH) for y in (ya, yb, yc, yd)]
    m = _merge(ys, r2, lw["w_branch_out"])
    out = _out_proj(x2, m, lw["w_out"], final_g, final_norm)
    return out.reshape(bsz, seq_len, D_MODEL)


def kernel(x_prompt, x_sample, norm_g, w_in, conv_w, conv_b, conv_ln_g, conv_ln_b, w_a_out, pool_w, pool_scale, w_b_out, hg_lb, hg_norm_g, w_c_out, s5_a_re, s5_a_im, s5_log_dt, s5_b_re, s5_b_im, s5_c_re, s5_c_im, s5_d, s5_glu_w, s5_glu_b, w_d_out, w_out, final_g):
    lb_all = _lower_bounds(hg_lb)
    nm, mk = _hg_tables()
    nm_bf = jnp.asarray(nm, BF16)
    mk_bf = jnp.asarray(mk, BF16)
    perm_bf = jnp.asarray(_s5_perm(), BF16)
    layers = []
    for l in range(DEPTH):
        prep = _s5_prep(s5_a_re[l], s5_a_im[l], s5_log_dt[l], s5_b_re[l], s5_b_im[l])
        layers.append(dict(
            norm_g=norm_g[l][None], w_in=w_in[l].astype(BF16),
            conv_w=conv_w[l], conv_b=conv_b[l][None], conv_ln_g=conv_ln_g[l][None], conv_ln_b=conv_ln_b[l][None],
            pool_w=pool_w[l].astype(BF16), pool_scale=pool_scale[l][None],
            lb=lb_all[l], hg_nm=nm_bf, hg_mk=mk_bf, hg_norm_g=hg_norm_g[l][None],
            s5=_s5_operators(prep, s5_c_re[l], s5_c_im[l]), s5_d=s5_d[l][None],
            s5_perm=perm_bf,
            s5_glu_w=s5_glu_w[l].astype(BF16), s5_glu_b=s5_glu_b[l][None],
            w_branch_out=jnp.stack([w_a_out[l], w_b_out[l], w_c_out[l], w_d_out[l]]).astype(BF16),
            w_out=w_out[l].astype(BF16)))
    fg = final_g[None]

    def run(x):
        for l in range(DEPTH):
            x = _layer(x, layers[l], fg, l == DEPTH - 1)
        return x

    return (run(x_prompt), run(x_sample))
```
